```python
import jax, jax.numpy as jnp
from jax import lax
import numpy as np

D_MODEL = 1024
BATCH = 16
SEQ = 2048
DEPTH = 2
DEC_BATCH = 8
DEC_SEQ = 4096
PAST_LEN = 128

HEAD_DIM = 64
N_MIX_HEADS = D_MODEL // HEAD_DIM
A_HEADS = N_MIX_HEADS // 4
B_HEADS = N_MIX_HEADS // 4
C_HEADS = N_MIX_HEADS // 2
C_KV_HEADS = C_HEADS // 4
A_WIDTH = A_HEADS * HEAD_DIM
B_WIDTH = B_HEADS * HEAD_DIM
C_WIDTH = C_HEADS * HEAD_DIM
C_KV_WIDTH = C_KV_HEADS * HEAD_DIM
DILATED_PAIRS = ((128, 1), (512, 4), (2048, 16))
ROPE_THETA = 500000.0
ROPE_DIMS = HEAD_DIM // 4
AXIAL_THETA = 10000.0
GRID_W = 64
Q_BLOCK = 128
MLSTM_CHUNK = 64
N_MEM = 256
X_HEADS = 4
X_HEAD_DIM = 64
X_WIDTH = X_HEADS * X_HEAD_DIM
D_FF = ((8 * D_MODEL) // 3 + 63) // 64 * 64
CONV_W = 3
RMS_EPS = 1e-6
NEG_INF = -1e30
IN_SPLITS = (A_WIDTH, A_WIDTH, A_WIDTH,
             2 * B_WIDTH, B_WIDTH, B_WIDTH, 4 * B_HEADS,
             C_WIDTH, C_KV_WIDTH, C_KV_WIDTH)
IN_WIDTH = sum(IN_SPLITS)

kernel_name = "hybrid_bidir_encoder_dilated_mlstm_axial_gqa"


def rmsnorm(x, g):
    xf = x.astype(jnp.float32)
    y = xf * lax.rsqrt(jnp.mean(xf * xf, axis=-1, keepdims=True) + RMS_EPS)
    return (y * g.astype(jnp.float32)).astype(x.dtype)


def rope_cos_sin(pos, dim, theta):
    inv = theta ** (-jnp.arange(0, dim, 2, dtype=jnp.float32) / dim)
    ang = pos.astype(jnp.float32)[:, None] * inv[None, :]
    return jnp.cos(ang), jnp.sin(ang)


def rotate(x, cos, sin):
    xf = x.astype(jnp.float32)
    half = xf.shape[-1] // 2
    x1, x2 = xf[..., :half], xf[..., half:]
    c, s = cos[None, :, None, :], sin[None, :, None, :]
    return jnp.concatenate([x1 * c - x2 * s, x1 * s + x2 * c], axis=-1).astype(x.dtype)


def partial_rope(x, rope):
    return jnp.concatenate([rotate(x[..., :ROPE_DIMS], *rope), x[..., ROPE_DIMS:]], axis=-1)


def axial_rope(x, rope_row, rope_col):
    half = HEAD_DIM // 2
    return jnp.concatenate([rotate(x[..., :half], *rope_row), rotate(x[..., half:], *rope_col)], axis=-1)


def dwconv3(x, w, b):
    xp = jnp.pad(x, ((0, 0), (1, 1), (0, 0)))
    return xp[:, :-2] * w[0] + xp[:, 1:-1] * w[1] + xp[:, 2:] * w[2] + b


def dilated_branch(q, k, v, window, dilation):
    B, S, H, Dh = q.shape
    R = window // (2 * dilation)
    L = S // dilation
    nblk = -(-L // R)
    Lp = nblk * R

    def strided(t):
        return t.reshape(B, L, dilation, H, Dh).transpose(0, 2, 1, 3, 4)

    qb = jnp.pad(strided(q), ((0, 0), (0, 0), (0, Lp - L), (0, 0), (0, 0))).reshape(B, dilation, nblk, R, H, Dh)
    pad_k = ((0, 0), (0, 0), (R, Lp - L + R), (0, 0), (0, 0))
    ks = jnp.pad(strided(k), pad_k).reshape(B, dilation, nblk + 2, R, H, Dh)
    vs = jnp.pad(strided(v), pad_k).reshape(B, dilation, nblk + 2, R, H, Dh)
    band = lambda t: jnp.concatenate([t[:, :, :-2], t[:, :, 1:-1], t[:, :, 2:]], axis=3)
    kw, vw = band(ks), band(vs)
    qpos = jnp.arange(nblk)[:, None] * R + jnp.arange(R)[None, :]
    kpos = jnp.arange(nblk)[:, None] * R - R + jnp.arange(3 * R)[None, :]
    kp = kpos[:, None, :]
    mask = (jnp.abs(kp - qpos[:, :, None]) <= R) & (kp >= 0) & (kp < L)
    s = jnp.einsum('bdnqhe,bdnkhe->bdnhqk', qb, kw).astype(jnp.float32)
    s = jnp.where(mask[:, None], s, NEG_INF)
    lse = jax.nn.logsumexp(s, axis=-1)
    p = jnp.exp(s - lse[..., None]).astype(v.dtype)
    o = jnp.einsum('bdnhqk,bdnkhe->bdnqhe', p, vw)
    o = o.reshape(B, dilation, Lp, H, Dh)[:, :, :L].transpose(0, 2, 1, 3, 4).reshape(B, S, H, Dh)
    lse = lse.transpose(0, 1, 2, 4, 3).reshape(B, dilation, Lp, H)[:, :, :L].transpose(0, 2, 1, 3).reshape(B, S, H)
    return o, lse


def dilated_attention(q, k, v):
    outs, lses = [], []
    for window, dilation in DILATED_PAIRS:
        o, lse = dilated_branch(q, k, v, window, dilation)
        outs.append(o)
        lses.append(lse)
    wts = jax.nn.softmax(jnp.stack(lses, axis=0), axis=0)
    y = jnp.einsum('nbsh,nbshd->bshd', wts, jnp.stack(outs, axis=0).astype(jnp.float32))
    return y.astype(v.dtype)


def mlstm_chunkwise(q, k, v, ig, lf):
    N, H, S, Dh = q.shape
    nc = S // MLSTM_CHUNK

    def chunked(t):
        return jnp.moveaxis(t.reshape((N, H, nc, MLSTM_CHUNK) + t.shape[3:]), 2, 0)

    causal = jnp.tril(jnp.ones((MLSTM_CHUNK, MLSTM_CHUNK), dtype=bool))

    def step(carry, inp):
        C, n, m = carry
        qc, kc, vc, igc, lfc = inp
        b = jnp.cumsum(lfc, axis=-1)
        a = b + m[..., None]
        logw = jnp.where(causal, b[..., :, None] - b[..., None, :] + igc[..., None, :], NEG_INF)
        m_loc = jnp.maximum(a, jnp.max(logw, axis=-1))
        w_state = jnp.exp(a - m_loc)
        w_intra = jnp.exp(logw - m_loc[..., None]) * jnp.einsum('nhld,nhsd->nhls', qc, kc)
        num = w_state[..., None] * jnp.einsum('nhld,nhde->nhle', qc, C) + jnp.einsum('nhls,nhse->nhle', w_intra, vc)
        den = w_state * jnp.einsum('nhld,nhd->nhl', qc, n) + jnp.sum(w_intra, axis=-1)
        h = num / jnp.maximum(jnp.abs(den), jnp.exp(-m_loc))[..., None]
        g_end = b[..., -1:] - b + igc
        a_end = b[..., -1] + m
        m_new = jnp.maximum(a_end, jnp.max(g_end, axis=-1))
        decay = jnp.exp(a_end - m_new)
        w_end = jnp.exp(g_end - m_new[..., None])
        C_new = decay[..., None, None] * C + jnp.einsum('nhl,nhld,nhle->nhde', w_end, kc, vc)
        n_new = decay[..., None] * n + jnp.einsum('nhl,nhld->nhd', w_end, kc)
        return (C_new, n_new, m_new), h

    init = (jnp.zeros((N, H, Dh, Dh), jnp.float32), jnp.zeros((N, H, Dh), jnp.float32), jnp.zeros((N, H), jnp.float32))
    _, h = lax.scan(step, init, (chunked(q), chunked(k), chunked(v), chunked(ig), chunked(lf)))
    return jnp.moveaxis(h, 0, 2).reshape(N, H, S, Dh)


def mlstm_mixer(qk, v, o_pre, gates, conv_w, conv_b, ig_b, fg_b, norm_g):
    B, S, _ = v.shape
    qk = jax.nn.silu(dwconv3(qk, conv_w, conv_b))
    to_heads = lambda t: t.astype(jnp.float32).reshape(B, S, B_HEADS, HEAD_DIM).transpose(0, 2, 1, 3)
    q = to_heads(qk[..., :B_WIDTH])
    k = to_heads(qk[..., B_WIDTH:]) * HEAD_DIM ** -0.5
    vh = to_heads(v)
    g = gates.astype(jnp.float32).reshape(B, S, 2, 2, B_HEADS)
    ig = (g[:, :, :, 0] + ig_b.astype(jnp.float32)).transpose(2, 0, 3, 1)
    lf = jax.nn.log_sigmoid(g[:, :, :, 1] + fg_b.astype(jnp.float32)).transpose(2, 0, 3, 1)
    flip = lambda t: jnp.flip(t, axis=2)
    both = lambda t: jnp.concatenate([t, flip(t)], axis=0)
    hcat = mlstm_chunkwise(both(q), both(k), both(vh),
                           jnp.concatenate([ig[0], flip(ig[1])], axis=0),
                           jnp.concatenate([lf[0], flip(lf[1])], axis=0))
    hs = (hcat[:B] + flip(hcat[B:])).transpose(0, 2, 1, 3)
    hs = hs * lax.rsqrt(jnp.mean(hs * hs, axis=-1, keepdims=True) + RMS_EPS)
    hs = hs * norm_g.astype(jnp.float32).reshape(B_HEADS, HEAD_DIM)
    return (jax.nn.sigmoid(o_pre.astype(jnp.float32)) * hs.reshape(B, S, B_WIDTH)).astype(v.dtype)


def blocked_gqa(q, k, v):
    B, S, _, Dh = q.shape
    G = C_HEADS // C_KV_HEADS
    nb = S // Q_BLOCK
    qb = (q * Dh ** -0.5).reshape(B, nb, Q_BLOCK, C_KV_HEADS, G, Dh).transpose(1, 0, 2, 3, 4, 5)

    def attend(qblk):
        s = jnp.einsum('bqkgd,bskd->bkgqs', qblk, k).astype(jnp.float32)
        p = jax.nn.softmax(s, axis=-1).astype(v.dtype)
        return jnp.einsum('bkgqs,bskd->bqkgd', p, v)

    o = lax.map(attend, qb)
    return o.transpose(1, 0, 2, 3, 4, 5).reshape(B, S, C_WIDTH)


def parallel_mixers(h, w_in, conv_w, conv_b, ig_b, fg_b, m_norm_g, qk_g, w_out, rope_p, rope_row, rope_col):
    B, S, _ = h.shape
    z = h @ w_in
    idx = np.cumsum(IN_SPLITS)[:-1].tolist()
    a_q, a_k, a_v, b_qk, b_v, b_o, b_g, c_q, c_k, c_v = jnp.split(z, idx, axis=-1)
    heads = lambda t, n: t.reshape(B, S, n, HEAD_DIM)
    qa = partial_rope(heads(a_q, A_HEADS) * HEAD_DIM ** -0.5, rope_p)
    ka = partial_rope(heads(a_k, A_HEADS), rope_p)
    ya = dilated_attention(qa, ka, heads(a_v, A_HEADS)).reshape(B, S, A_WIDTH)
    yb = mlstm_mixer(b_qk, b_v, b_o, b_g, conv_w, conv_b, ig_b, fg_b, m_norm_g)
    qc = axial_rope(rmsnorm(heads(c_q, C_HEADS), qk_g[0]), rope_row, rope_col)
    kc = axial_rope(rmsnorm(heads(c_k, C_KV_HEADS), qk_g[1]), rope_row, rope_col)
    yc = blocked_gqa(qc, kc, heads(c_v, C_KV_HEADS))
    return jnp.concatenate([ya, yb, yc], axis=-1) @ w_out


def memory_cross_attention(h, mem, mem_g, w_q, w_kv, w_o):
    B, S, _ = h.shape
    M = mem.shape[1]
    q = (h @ w_q).reshape(B, S, X_HEADS, X_HEAD_DIM) * X_HEAD_DIM ** -0.5
    kv = rmsnorm(mem, mem_g) @ w_kv
    k = kv[..., :X_WIDTH].reshape(B, M, X_HEADS, X_HEAD_DIM)
    v = kv[..., X_WIDTH:].reshape(B, M, X_HEADS, X_HEAD_DIM)
    s = jnp.einsum('bshd,bmhd->bhsm', q, k).astype(jnp.float32)
    p = jax.nn.softmax(s, axis=-1).astype(v.dtype)
    o = jnp.einsum('bhsm,bmhd->bshd', p, v).reshape(B, S, X_WIDTH)
    return o @ w_o


def conv_ffn(h, w_up, conv_w, conv_b, w_down):
    u = h @ w_up
    gate, val = u[..., :D_FF], u[..., D_FF:]
    gate = dwconv3(gate, conv_w, conv_b)
    return (jax.nn.silu(gate) * val) @ w_down


def encoder_trunk(x, mem, norm_mix_g, w_in, mlstm_conv_w, mlstm_conv_b, mlstm_igate_b, mlstm_fgate_b, mlstm_norm_g,
                  qk_norm_g, w_out, norm_x_g, norm_mem_g, w_xq, w_xkv, w_xo, norm_ffn_g, w_ffn_up, ffn_conv_w,
                  ffn_conv_b, w_ffn_down, final_norm_g):
    B, S, _ = x.shape
    rows = S // GRID_W
    rope_p = rope_cos_sin(jnp.arange(S), ROPE_DIMS, ROPE_THETA)
    row_idx = jnp.repeat(jnp.arange(rows), GRID_W)
    col_idx = jnp.tile(jnp.arange(GRID_W), rows)
    rope_row = rope_cos_sin(row_idx, HEAD_DIM // 2, AXIAL_THETA)
    rope_col = rope_cos_sin(col_idx, HEAD_DIM // 2, AXIAL_THETA)
    for l in range(DEPTH):
        x = x + parallel_mixers(rmsnorm(x, norm_mix_g[l]), w_in[l], mlstm_conv_w[l], mlstm_conv_b[l],
                                mlstm_igate_b[l], mlstm_fgate_b[l], mlstm_norm_g[l], qk_norm_g[l], w_out[l],
                                rope_p, rope_row, rope_col)
        x = x + memory_cross_attention(rmsnorm(x, norm_x_g[l]), mem, norm_mem_g[l], w_xq[l], w_xkv[l], w_xo[l])
        x = x + conv_ffn(rmsnorm(x, norm_ffn_g[l]), w_ffn_up[l], ffn_conv_w[l], ffn_conv_b[l], w_ffn_down[l])
    return rmsnorm(x, final_norm_g)


def setup_inputs(seed: int = 0) -> dict:
    key = jax.random.key(seed)
    ks = jax.random.split(key, 25)
    nrm = lambda k, shape, scale: jax.random.normal(k, shape, jnp.float32) * scale
    gain = lambda k, shape: 1.0 + 0.1 * jax.random.normal(k, shape, jnp.float32)
    fgate = jnp.linspace(3.0, 6.0, B_HEADS)[None, None, :] + nrm(ks[10], (DEPTH, 2, B_HEADS), 0.1)
    return {
        "x_prompt": nrm(ks[0], (BATCH, SEQ, D_MODEL), 1.0),
        "x_sample": nrm(ks[1], (DEC_BATCH, DEC_SEQ, D_MODEL), 1.0),
        "mem_prompt": nrm(ks[2], (BATCH, N_MEM, D_MODEL), 1.0),
        "mem_sample": nrm(ks[3], (DEC_BATCH, N_MEM, D_MODEL), 1.0),
        "norm_mix_g": gain(ks[4], (DEPTH, D_MODEL)),
        "w_in": nrm(ks[5], (DEPTH, D_MODEL, IN_WIDTH), D_MODEL ** -0.5),
        "mlstm_conv_w": nrm(ks[6], (DEPTH, CONV_W, 2 * B_WIDTH), CONV_W ** -0.5),
        "mlstm_conv_b": nrm(ks[7], (DEPTH, 2 * B_WIDTH), 0.02),
        "mlstm_igate_b": nrm(ks[8], (DEPTH, 2, B_HEADS), 0.1),
        "mlstm_fgate_b": fgate,
        "mlstm_norm_g": gain(ks[9], (DEPTH, B_WIDTH)),
        "qk_norm_g": gain(ks[11], (DEPTH, 2, HEAD_DIM)),
        "w_out": nrm(ks[12], (DEPTH, D_MODEL, D_MODEL), D_MODEL ** -0.5),
        "norm_x_g": gain(ks[13], (DEPTH, D_MODEL)),
        "norm_mem_g": gain(ks[14], (DEPTH, D_MODEL)),
        "w_xq": nrm(ks[15], (DEPTH, D_MODEL, X_WIDTH), D_MODEL ** -0.5),
        "w_xkv": nrm(ks[16], (DEPTH, D_MODEL, 2 * X_WIDTH), D_MODEL ** -0.5),
        "w_xo": nrm(ks[17], (DEPTH, X_WIDTH, D_MODEL), X_WIDTH ** -0.5),
        "norm_ffn_g": gain(ks[18], (DEPTH, D_MODEL)),
        "w_ffn_up": nrm(ks[19], (DEPTH, D_MODEL, 2 * D_FF), D_MODEL ** -0.5),
        "ffn_conv_w": nrm(ks[20], (DEPTH, CONV_W, D_FF), CONV_W ** -0.5),
        "ffn_conv_b": nrm(ks[21], (DEPTH, D_FF), 0.02),
        "w_ffn_down": nrm(ks[22], (DEPTH, D_FF, D_MODEL), D_FF ** -0.5),
        "final_norm_g": gain(ks[23], (D_MODEL,)),
    }


def reference(x_prompt, x_sample, mem_prompt, mem_sample, norm_mix_g, w_in, mlstm_conv_w, mlstm_conv_b,
              mlstm_igate_b, mlstm_fgate_b, mlstm_norm_g, qk_norm_g, w_out, norm_x_g, norm_mem_g, w_xq, w_xkv,
              w_xo, norm_ffn_g, w_ffn_up, ffn_conv_w, ffn_conv_b, w_ffn_down, final_norm_g):
    y_prompt = encoder_trunk(x_prompt, mem_prompt, norm_mix_g, w_in, mlstm_conv_w, mlstm_conv_b, mlstm_igate_b,
                             mlstm_fgate_b, mlstm_norm_g, qk_norm_g, w_out, norm_x_g, norm_mem_g, w_xq, w_xkv, w_xo,
                             norm_ffn_g, w_ffn_up, ffn_conv_w, ffn_conv_b, w_ffn_down, final_norm_g)
    y_sample = encoder_trunk(x_sample, mem_sample, norm_mix_g, w_in, mlstm_conv_w, mlstm_conv_b, mlstm_igate_b,
                             mlstm_fgate_b, mlstm_norm_g, qk_norm_g, w_out, norm_x_g, norm_mem_g, w_xq, w_xkv, w_xo,
                             norm_ffn_g, w_ffn_up, ffn_conv_w, ffn_conv_b, w_ffn_down, final_norm_g)
    return (y_prompt, y_sample)
```

```python
import functools

import numpy as np
import jax
import jax.numpy as jnp
from jax import lax
from jax.experimental import pallas as pl
from jax.experimental.pallas import tpu as pltpu

F32 = jnp.float32
BF16 = jnp.bfloat16

LANES = 128
SUBLANES = 8
VMEM_LIMIT_BYTES = 56 * 1024 * 1024

D_MODEL = 1024
HEAD_DIM = 64
A_HEADS = 4
B_HEADS = 4
C_HEADS = 8
C_KV_HEADS = 2
A_WIDTH = A_HEADS * HEAD_DIM
B_WIDTH = B_HEADS * HEAD_DIM
C_WIDTH = C_HEADS * HEAD_DIM
C_KV_WIDTH = C_KV_HEADS * HEAD_DIM
DILATED_PAIRS = ((128, 1), (512, 4), (2048, 16))
BAND = 64
ROPE_THETA = 500000.0
ROPE_DIMS = HEAD_DIM // 4
AXIAL_THETA = 10000.0
GRID_W = 64
N_MEM = 256
X_HEADS = 4
X_WIDTH = 256
D_FF = 2752
D_FF_PAD = 2816
RMS_EPS = 1e-6
NEG_INF = -1e30
Q_SCALE = HEAD_DIM ** -0.5

COL_A = 0
COL_BQK = COL_A + 3 * A_WIDTH
COL_BV = COL_BQK + 2 * B_WIDTH
COL_BO = COL_BV + B_WIDTH
COL_CQ = COL_BO + B_WIDTH
COL_CK = COL_CQ + C_WIDTH
COL_CV = COL_CK + C_KV_WIDTH
COL_G = COL_CV + C_KV_WIDTH
IN_COLS = COL_G + LANES
N_GATE = 2 * B_HEADS

TQ_DIL = 128
MLSTM_CHUNK = 256
TQ_GQA = 256
KC_GQA = 512
FF_CHUNK = 256


def _dot(a, b):
    return jnp.dot(a, b, preferred_element_type=F32)


def _dot_nt(a, b):
    return lax.dot_general(a, b, (((1,), (1,)), ((), ())), preferred_element_type=F32)


def _split2(x):
    hi = x.astype(BF16)
    return hi, (x - hi.astype(F32)).astype(BF16)


def _split3(x):
    hi = x.astype(BF16)
    r = x - hi.astype(F32)
    mid = r.astype(BF16)
    return hi, mid, (r - mid.astype(F32)).astype(BF16)


def _head_mean(xx, bd):
    hi, lo = _split2(xx)
    return _dot(hi, bd) + _dot(lo, bd)


def _rms(x, g):
    return x * lax.rsqrt(jnp.mean(x * x, axis=-1, keepdims=True) + RMS_EPS) * g


def _lane_iota(shape):
    return lax.broadcasted_iota(jnp.int32, shape, len(shape) - 1)


def _head_mask(width, h, dtype):
    lane = _lane_iota((1, width))
    return ((lane >= h * HEAD_DIM) & (lane < (h + 1) * HEAD_DIM)).astype(dtype)


def _rope(x, tab_ref, shift):
    w = x.shape[1]
    rep = w // LANES
    tile = lambda t: jnp.concatenate([t] * rep, axis=1) if rep > 1 else t
    c, sm, sp = tile(tab_ref[0]), tile(tab_ref[1]), tile(tab_ref[2])
    return x * c + pltpu.roll(x, w - shift, axis=1) * sm + pltpu.roll(x, shift, axis=1) * sp


def _shift_rows(z, prev_row, next_row):
    n = z.shape[0]
    row = lax.broadcasted_iota(jnp.int32, z.shape, 0)
    zp = jnp.where(row == 0, prev_row, pltpu.roll(z, 1, axis=0))
    zn = jnp.where(row == n - 1, next_row, pltpu.roll(z, n - 1, axis=0))
    return zp, zn


def _params(sem):
    return pltpu.CompilerParams(dimension_semantics=sem, vmem_limit_bytes=VMEM_LIMIT_BYTES)


def _const_spec(shape):
    n = len(shape)
    return pl.BlockSpec(shape, lambda *_: (0,) * n)


def _mem_kv_kernel(mem_ref, g_ref, w_ref, kt_ref, v_ref):
    hn = _rms(mem_ref[0], g_ref[...]).astype(BF16)
    kv = _dot(hn, w_ref[...])
    kt_ref[0] = kv[:, :X_WIDTH].T.astype(BF16)
    v_ref[0] = kv[:, X_WIDTH:].astype(BF16)


def _mem_kv(mem, g, w):
    b = mem.shape[0]
    return pl.pallas_call(
        _mem_kv_kernel,
        grid=(b,),
        in_specs=[pl.BlockSpec((1, N_MEM, D_MODEL), lambda i: (i, 0, 0)),
                  _const_spec((1, D_MODEL)), _const_spec((D_MODEL, 2 * X_WIDTH))],
        out_specs=[pl.BlockSpec((1, X_WIDTH, N_MEM), lambda i: (i, 0, 0)),
                   pl.BlockSpec((1, N_MEM, X_WIDTH), lambda i: (i, 0, 0))],
        out_shape=[jax.ShapeDtypeStruct((b, X_WIDTH, N_MEM), BF16),
                   jax.ShapeDtypeStruct((b, N_MEM, X_WIDTH), BF16)],
        compiler_params=_params(("arbitrary",)),
        name="mem_kv",
    )(mem, g, w)


def _in_proj_kernel(x_ref, xp_ref, xn_ref, g_ref, w_ref, cw_ref, cb_ref, ks_ref, ta_ref, tc_ref, qg_ref, kg_ref,
                    bd_ref, a_ref, bqk_ref, bv_ref, bo_ref, bg_ref, cq_ref, ckv_ref):
    i = pl.program_id(1)
    last = pl.num_programs(1) - 1
    g = g_ref[...]
    hn = _rms(x_ref[0], g).astype(BF16)

    za = _dot(hn, w_ref[:, COL_A:COL_BQK])
    qk = _rope(za[:, :2 * A_WIDTH], ta_ref, ROPE_DIMS // 2)
    a_ref[0, :, :2 * A_WIDTH] = qk.astype(BF16)
    a_ref[0, :, 2 * A_WIDTH:] = za[:, 2 * A_WIDTH:].astype(BF16)

    wb = w_ref[:, COL_BQK:COL_BV]
    zb = _dot(hn, wb)
    zprev = _dot(_rms(xp_ref[0], g).astype(BF16), wb)[SUBLANES - 1:SUBLANES] * (i > 0).astype(F32)
    znext = _dot(_rms(xn_ref[0], g).astype(BF16), wb)[0:1] * (i < last).astype(F32)
    zp, zn = _shift_rows(zb, zprev, znext)
    conv = zp * cw_ref[0:1] + zb * cw_ref[1:2] + zn * cw_ref[2:3] + cb_ref[...]
    bqk_ref[0] = (conv * jax.nn.sigmoid(conv) * ks_ref[...]).astype(BF16)
    zrest = _dot(hn, w_ref[:, COL_BV:COL_CQ])
    bv_ref[0] = zrest[:, :B_WIDTH].astype(BF16)
    bo_ref[0] = zrest[:, B_WIDTH:]
    bg_ref[0] = _dot(hn, w_ref[:, COL_G:IN_COLS])

    zc = _dot(hn, w_ref[:, COL_CQ:COL_G])
    cq = zc[:, :C_WIDTH]
    cq = cq * lax.rsqrt(_head_mean(cq * cq, bd_ref[...]) + RMS_EPS) * qg_ref[...]
    cq_ref[0] = _rope(cq, tc_ref, HEAD_DIM // 4).astype(BF16)
    ck = zc[:, C_WIDTH:C_WIDTH + C_KV_WIDTH]
    ck = ck * lax.rsqrt(_head_mean(ck * ck, bd_ref[:C_KV_WIDTH, :C_KV_WIDTH]) + RMS_EPS) * kg_ref[...]
    ckv_ref[0, :, :C_KV_WIDTH] = _rope(ck, tc_ref, HEAD_DIM // 4).astype(BF16)
    ckv_ref[0, :, C_KV_WIDTH:] = zc[:, C_WIDTH + C_KV_WIDTH:].astype(BF16)


def _in_proj(x, g, w, cw, cb, kscale, ta, tc, qg, kg, bd, tm):
    b, s, _ = x.shape
    nt = s // tm
    hb = tm // SUBLANES
    tok = lambda width: pl.BlockSpec((1, tm, width), lambda bi, i: (bi, i, 0))
    out_widths = (3 * A_WIDTH, 2 * B_WIDTH, B_WIDTH, B_WIDTH, LANES, C_WIDTH, 2 * C_KV_WIDTH)
    out_dtypes = (BF16, BF16, BF16, F32, F32, BF16, BF16)
    return pl.pallas_call(
        _in_proj_kernel,
        grid=(b, nt),
        in_specs=[tok(D_MODEL),
                  pl.BlockSpec((1, SUBLANES, D_MODEL), lambda bi, i: (bi, jnp.maximum(i * hb - 1, 0), 0)),
                  pl.BlockSpec((1, SUBLANES, D_MODEL), lambda bi, i: (bi, jnp.minimum((i + 1) * hb, s // SUBLANES - 1), 0)),
                  _const_spec((1, D_MODEL)), _const_spec((D_MODEL, IN_COLS)),
                  _const_spec((3, 2 * B_WIDTH)), _const_spec((1, 2 * B_WIDTH)), _const_spec((1, 2 * B_WIDTH)),
                  pl.BlockSpec((3, tm, LANES), lambda bi, i: (0, i, 0)),
                  pl.BlockSpec((3, tm, LANES), lambda bi, i: (0, i, 0)),
                  _const_spec((1, C_WIDTH)), _const_spec((1, C_KV_WIDTH)), _const_spec((C_WIDTH, C_WIDTH))],
        out_specs=[tok(wd) for wd in out_widths],
        out_shape=[jax.ShapeDtypeStruct((b, s, wd), dt) for wd, dt in zip(out_widths, out_dtypes)],
        compiler_params=_params(("arbitrary", "arbitrary")),
        name="in_proj",
    )(x, x, x, g, w, cw, cb, kscale, ta, tc, qg, kg, bd)


def _dilated_kernel(a_ref, o_ref, l_ref, *, length, n_res, win):
    n_blk = length // TQ_DIL
    masks = [_head_mask(A_WIDTH, h, BF16) for h in range(A_HEADS)]
    lane = _lane_iota((1, A_WIDTH))
    rel = (lax.broadcasted_iota(jnp.int32, (1, TQ_DIL, win), 2)
           - lax.broadcasted_iota(jnp.int32, (1, TQ_DIL, win), 1))

    def pick(parts):
        out = parts[A_HEADS - 1]
        for h in range(A_HEADS - 2, -1, -1):
            out = jnp.where(lane < (h + 1) * HEAD_DIM, parts[h], out)
        return out

    for r in range(n_res):
        cq = r * 3 * A_WIDTH

        def block(jb, carry, cq=cq, r=r):
            j0 = pl.multiple_of(jb * TQ_DIL, TQ_DIL)
            ws = pl.multiple_of(jnp.clip(j0 - BAND, 0, length - win), BAND)
            q = a_ref[0, pl.ds(j0, TQ_DIL), cq:cq + A_WIDTH]
            kw = a_ref[0, pl.ds(ws, win), cq + A_WIDTH:cq + 2 * A_WIDTH]
            vw = a_ref[0, pl.ds(ws, win), cq + 2 * A_WIDTH:cq + 3 * A_WIDTH]
            q4 = jnp.concatenate([q * m for m in masks], axis=0)
            s = _dot_nt(q4, kw).reshape(A_HEADS, TQ_DIL, win)
            s = jnp.where(jnp.abs(rel + (ws - j0)) <= BAND, s, NEG_INF)
            m = jnp.max(s, axis=-1, keepdims=True)
            p = jnp.exp(s - m)
            l = jnp.sum(p, axis=-1, keepdims=True)
            pn = (p / l).astype(BF16).reshape(A_HEADS * TQ_DIL, win)
            o4 = _dot(pn, vw).reshape(A_HEADS, TQ_DIL, A_WIDTH)
            lse = m + jnp.log(l)
            o_ref[0, pl.ds(j0, TQ_DIL), r * A_WIDTH:(r + 1) * A_WIDTH] = pick([o4[h] for h in range(A_HEADS)])
            l_ref[0, pl.ds(j0, TQ_DIL), r * A_WIDTH:(r + 1) * A_WIDTH] = pick(
                [jnp.broadcast_to(lse[h], (TQ_DIL, A_WIDTH)) for h in range(A_HEADS)])
            return carry

        lax.fori_loop(0, n_blk, block, 0)


def _dilated_branch(a, dilation):
    b, s, _ = a.shape
    length = s // dilation
    n_res = max(1, min(dilation, 2048 // length))
    win = min(length, TQ_DIL + 2 * BAND)
    av = a.reshape(b, length, dilation * 3 * A_WIDTH)
    out_spec = pl.BlockSpec((1, length, n_res * A_WIDTH), lambda bi, ri: (bi, 0, ri))
    o, l = pl.pallas_call(
        functools.partial(_dilated_kernel, length=length, n_res=n_res, win=win),
        grid=(b, dilation // n_res),
        in_specs=[pl.BlockSpec((1, length, n_res * 3 * A_WIDTH), lambda bi, ri: (bi, 0, ri))],
        out_specs=[out_spec, out_spec],
        out_shape=[jax.ShapeDtypeStruct((b, length, dilation * A_WIDTH), F32)] * 2,
        compiler_params=_params(("arbitrary", "arbitrary")),
        name=f"dilated_d{dilation}",
    )(av)
    return o.reshape(b, s, A_WIDTH), l.reshape(b, s, A_WIDTH)


def _cummax_rows(x, reverse):
    n = x.shape[0]
    row = lax.broadcasted_iota(jnp.int32, x.shape, 0)
    step = 1
    while step < n:
        if reverse:
            x = jnp.maximum(x, jnp.where(row < n - step, pltpu.roll(x, n - step, axis=0), NEG_INF))
        else:
            x = jnp.maximum(x, jnp.where(row >= step, pltpu.roll(x, step, axis=0), NEG_INF))
        step *= 2
    return x


def _mlstm_kernel(qk_ref, v_ref, o_ref, g_ref, gb_ref, ng_ref, tl_ref, tu_ref, bd_ref, out_ref,
                  hb_ref, c_ref, n_ref, m_ref):
    lc = MLSTM_CHUNK
    n_chunks = qk_ref.shape[1] // lc
    lane = _lane_iota((1, B_WIDTH))
    head_masks = [_head_mask(B_WIDTH, h, BF16) for h in range(B_HEADS)]
    row_i = lax.broadcasted_iota(jnp.int32, (lc, lc), 0)
    col_j = lax.broadcasted_iota(jnp.int32, (lc, lc), 1)
    feat = lax.broadcasted_iota(jnp.int32, (B_WIDTH, B_WIDTH), 0) // HEAD_DIM
    bd_mask = feat == lax.broadcasted_iota(jnp.int32, (B_WIDTH, B_WIDTH), 1) // HEAD_DIM
    nfeat = lax.broadcasted_iota(jnp.int32, (B_WIDTH, LANES), 0) // HEAD_DIM
    nlane = lax.broadcasted_iota(jnp.int32, (B_WIDTH, LANES), 1)

    def spread(cols):
        out = cols[B_HEADS - 1]
        for h in range(B_HEADS - 2, -1, -1):
            out = jnp.where(lane < (h + 1) * HEAD_DIM, cols[h], out)
        return out

    def chunk(c, reverse):
        base = B_HEADS if reverse else 0
        r0 = pl.multiple_of(c * lc, lc)
        q = qk_ref[0, pl.ds(r0, lc), :B_WIDTH]
        k = qk_ref[0, pl.ds(r0, lc), B_WIDTH:]
        v = v_ref[0, pl.ds(r0, lc), :]
        gates = g_ref[0, pl.ds(r0, lc), :] + gb_ref[...]
        ig = gates
        lsig = jnp.minimum(gates, 0.0) - jnp.log1p(jnp.exp(-jnp.abs(gates)))
        lf = pltpu.roll(lsig, LANES - N_GATE, axis=1)
        tri = (tu_ref if reverse else tl_ref)[...]
        hi, mid, lo = _split3(lf)
        bcol = _dot(tri, hi) + _dot(tri, mid) + _dot(tri, lo)
        tot = jnp.sum(lf, axis=0, keepdims=True)
        rcol = ig - bcol
        gend = tot + rcol
        m_prev = m_ref[...]
        m_new = jnp.maximum(tot + m_prev, jnp.max(gend, axis=0, keepdims=True))
        wend = jnp.exp(gend - m_new)
        decay = jnp.exp(tot + m_prev - m_new)
        acol = -jnp.maximum(m_prev, _cummax_rows(rcol, reverse))
        wstate = jnp.exp(m_prev + acol)
        floor_ = jnp.exp(acol - bcol)
        r_t = rcol.T
        k_t = k.astype(F32).T.astype(BF16)
        q_c = _dot(q, c_ref[...].astype(BF16))
        q_n = _dot(q, n_ref[...].astype(BF16))
        mask = (col_j >= row_i) if reverse else (col_j <= row_i)
        hs = []
        for h in range(B_HEADS):
            dh = base + h
            sc = _dot(q * head_masks[h], k_t)
            e = jnp.exp(jnp.where(mask, acol[:, dh:dh + 1] + r_t[dh:dh + 1, :], NEG_INF))
            w = e * sc
            den = wstate[:, dh:dh + 1] * q_n[:, dh:dh + 1] + jnp.sum(w, axis=1, keepdims=True)
            num = wstate[:, dh:dh + 1] * q_c + _dot(w.astype(BF16), v)
            hs.append(num / jnp.maximum(jnp.abs(den), floor_[:, dh:dh + 1]))
        h_out = spread(hs)
        wend_full = spread([wend[:, base + h:base + h + 1] for h in range(B_HEADS)])
        decay_full = spread([decay[:, base + h:base + h + 1] for h in range(B_HEADS)])
        upd = _dot(k_t, (v.astype(F32) * wend_full).astype(BF16))
        c_ref[...] = decay_full * c_ref[...] + jnp.where(bd_mask, upd, 0.0)
        nupd = _dot(k_t, wend.astype(BF16))
        n_ref[...] = decay * n_ref[...] + jnp.where(nfeat == nlane - base, nupd, 0.0)
        m_ref[...] = m_new
        return r0, h_out

    def reset():
        c_ref[...] = jnp.zeros_like(c_ref)
        n_ref[...] = jnp.zeros_like(n_ref)
        m_ref[...] = jnp.zeros_like(m_ref)

    reset()

    def bwd(t, carry):
        r0, h_out = chunk(n_chunks - 1 - t, True)
        hb_ref[pl.ds(r0, lc), :] = h_out
        return carry

    lax.fori_loop(0, n_chunks, bwd, 0)
    reset()

    def fwd(t, carry):
        r0, h_out = chunk(t, False)
        hsum = h_out + hb_ref[pl.ds(r0, lc), :]
        hn = hsum * lax.rsqrt(_head_mean(hsum * hsum, bd_ref[...]) + RMS_EPS) * ng_ref[...]
        out_ref[0, pl.ds(r0, lc), :] = (jax.nn.sigmoid(o_ref[0, pl.ds(r0, lc), :]) * hn).astype(BF16)
        return carry

    lax.fori_loop(0, n_chunks, fwd, 0)


def _mlstm(bqk, bv, bo, bg, gbias, ng, tl, tu, bd):
    b, s, _ = bqk.shape
    row = lambda width: pl.BlockSpec((1, s, width), lambda i: (i, 0, 0))
    return pl.pallas_call(
        _mlstm_kernel,
        grid=(b,),
        in_specs=[row(2 * B_WIDTH), row(B_WIDTH), row(B_WIDTH), row(LANES),
                  _const_spec((1, LANES)), _const_spec((1, B_WIDTH)),
                  _const_spec((MLSTM_CHUNK, MLSTM_CHUNK)), _const_spec((MLSTM_CHUNK, MLSTM_CHUNK)),
                  _const_spec((B_WIDTH, B_WIDTH))],
        out_specs=row(B_WIDTH),
        out_shape=jax.ShapeDtypeStruct((b, s, B_WIDTH), BF16),
        scratch_shapes=[pltpu.VMEM((s, B_WIDTH), F32), pltpu.VMEM((B_WIDTH, B_WIDTH), F32),
                        pltpu.VMEM((B_WIDTH, LANES), F32), pltpu.VMEM((1, LANES), F32)],
        compiler_params=_params(("arbitrary",)),
        name="mlstm",
    )(bqk, bv, bo, bg, gbias, ng, tl, tu, bd)


def _gqa_kernel(q_ref, kv_ref, o_ref, kt_ref, vx_ref):
    s = kv_ref.shape[1]
    n_kc = s // KC_GQA
    lo = _lane_iota((1, LANES)) < HEAD_DIM
    lo_b = lo.astype(BF16)
    hi_b = 1.0 - lo_b

    @pl.when(pl.program_id(1) == 0)
    def _():
        def prep(c, carry):
            r0 = pl.multiple_of(c * KC_GQA, KC_GQA)
            kc = kv_ref[0, pl.ds(r0, KC_GQA), :C_KV_WIDTH].astype(F32)
            kr = pltpu.roll(kc, HEAD_DIM, axis=1)
            kt_ref[0, c] = jnp.where(lo, kc, kr).T.astype(BF16)
            kt_ref[1, c] = jnp.where(lo, kr, kc).T.astype(BF16)
            vc = kv_ref[0, pl.ds(r0, KC_GQA), C_KV_WIDTH:].astype(F32)
            vr = pltpu.roll(vc, HEAD_DIM, axis=1)
            vx_ref[0, pl.ds(r0, KC_GQA), :] = jnp.where(lo, vc, 0.0).astype(BF16)
            vx_ref[1, pl.ds(r0, KC_GQA), :] = jnp.where(lo, 0.0, vr).astype(BF16)
            vx_ref[2, pl.ds(r0, KC_GQA), :] = jnp.where(lo, vr, 0.0).astype(BF16)
            vx_ref[3, pl.ds(r0, KC_GQA), :] = jnp.where(lo, 0.0, vc).astype(BF16)
            return carry

        lax.fori_loop(0, n_kc, prep, 0)

    tq = q_ref.shape[1]
    for pair in range(C_HEADS // 2):
        grp = pair // (C_HEADS // C_KV_HEADS // 2)
        qp = q_ref[0, :, pair * LANES:(pair + 1) * LANES]
        out = jnp.zeros((tq, LANES), F32)
        for half in range(2):
            qm = qp * (lo_b if half == 0 else hi_b)

            def body(c, carry, qm=qm, grp=grp, half=half):
                m, l, acc = carry
                sc = _dot(qm, kt_ref[grp, c])
                m_new = jnp.maximum(m, jnp.max(sc, axis=1, keepdims=True))
                alpha = jnp.exp(m - m_new)
                p = jnp.exp(sc - m_new)
                l = alpha * l + jnp.sum(p, axis=1, keepdims=True)
                r0 = pl.multiple_of(c * KC_GQA, KC_GQA)
                acc = alpha * acc + _dot(p.astype(BF16), vx_ref[2 * grp + half, pl.ds(r0, KC_GQA), :])
                return m_new, l, acc

            init = (jnp.full((tq, 1), NEG_INF, F32), jnp.zeros((tq, 1), F32), jnp.zeros((tq, LANES), F32))
            _, l, acc = lax.fori_loop(0, n_kc, body, init)
            out = out + acc / l
        o_ref[0, :, pair * LANES:(pair + 1) * LANES] = out.astype(BF16)


def _gqa(cq, ckv):
    b, s, _ = cq.shape
    return pl.pallas_call(
        _gqa_kernel,
        grid=(b, s // TQ_GQA),
        in_specs=[pl.BlockSpec((1, TQ_GQA, C_WIDTH), lambda bi, i: (bi, i, 0)),
                  pl.BlockSpec((1, s, 2 * C_KV_WIDTH), lambda bi, i: (bi, 0, 0))],
        out_specs=pl.BlockSpec((1, TQ_GQA, C_WIDTH), lambda bi, i: (bi, i, 0)),
        out_shape=jax.ShapeDtypeStruct((b, s, C_WIDTH), BF16),
        scratch_shapes=[pltpu.VMEM((C_KV_HEADS, s // KC_GQA, LANES, KC_GQA), BF16),
                        pltpu.VMEM((2 * C_KV_HEADS, s, LANES), BF16)],
        compiler_params=_params(("arbitrary", "arbitrary")),
        name="gqa",
    )(cq, ckv)


def _out_cross_kernel(x_ref, o1_ref, l1_ref, o2_ref, l2_ref, o3_ref, l3_ref, yb_ref, yc_ref, wo_ref,
                      gx_ref, wq_ref, kt_ref, vm_ref, wxo_ref, out_ref):
    l1, l2, l3 = l1_ref[0], l2_ref[0], l3_ref[0]
    mx = jnp.maximum(jnp.maximum(l1, l2), l3)
    e1, e2, e3 = jnp.exp(l1 - mx), jnp.exp(l2 - mx), jnp.exp(l3 - mx)
    ya = (e1 * o1_ref[0] + e2 * o2_ref[0] + e3 * o3_ref[0]) / (e1 + e2 + e3)
    cat = jnp.concatenate([ya.astype(BF16), yb_ref[0], yc_ref[0]], axis=1)
    x1 = x_ref[0] + _dot(cat, wo_ref[...])

    tm = x1.shape[0]
    q = _dot(_rms(x1, gx_ref[...]).astype(BF16), wq_ref[...]).astype(BF16)
    q4 = jnp.concatenate([q * _head_mask(X_WIDTH, h, BF16) for h in range(X_HEADS)], axis=0)
    s = _dot(q4, kt_ref[0])
    m = jnp.max(s, axis=1, keepdims=True)
    p = jnp.exp(s - m)
    pn = (p / jnp.sum(p, axis=1, keepdims=True)).astype(BF16)
    o4 = _dot(pn, vm_ref[0])
    lane = _lane_iota((1, X_WIDTH))
    o = o4[(X_HEADS - 1) * tm:]
    for h in range(X_HEADS - 2, -1, -1):
        o = jnp.where(lane < (h + 1) * HEAD_DIM, o4[h * tm:(h + 1) * tm], o)
    out_ref[0] = x1 + _dot(o.astype(BF16), wxo_ref[...])


def _out_cross(x, branches, yb, yc, wo, gx, wq, kt, vm, wxo, tm):
    b, s, _ = x.shape
    tok = lambda width: pl.BlockSpec((1, tm, width), lambda bi, i: (bi, i, 0))
    per_b = lambda shape: pl.BlockSpec((1,) + shape, lambda bi, i: (bi, 0, 0))
    flat = [t for pair in branches for t in pair]
    return pl.pallas_call(
        _out_cross_kernel,
        grid=(b, s // tm),
        in_specs=[tok(D_MODEL)] + [tok(A_WIDTH)] * 6 + [tok(B_WIDTH), tok(C_WIDTH),
                  _const_spec((D_MODEL, D_MODEL)), _const_spec((1, D_MODEL)), _const_spec((D_MODEL, X_WIDTH)),
                  per_b((X_WIDTH, N_MEM)), per_b((N_MEM, X_WIDTH)), _const_spec((X_WIDTH, D_MODEL))],
        out_specs=tok(D_MODEL),
        out_shape=jax.ShapeDtypeStruct((b, s, D_MODEL), F32),
        compiler_params=_params(("arbitrary", "arbitrary")),
        name="out_cross",
    )(x, *flat, yb, yc, wo, gx, wq, kt, vm, wxo)


def _conv_ffn_kernel(x_ref, xp_ref, xn_ref, g_ref, wg_ref, wv_ref, cw_ref, cb_ref, wd_ref, fg_ref, out_ref, acc_ref,
                     *, final_norm):
    i = pl.program_id(1)
    last = pl.num_programs(1) - 1
    g = g_ref[...]
    x = x_ref[0]
    tm = x.shape[0]
    hn = _rms(x, g).astype(BF16)
    hp = (_rms(xp_ref[0], g) * (i > 0).astype(F32)).astype(BF16)
    hx = (_rms(xn_ref[0], g) * (i < last).astype(F32)).astype(BF16)
    h_ext = jnp.concatenate([hp, hn, hx], axis=0)
    for c in range(D_FF_PAD // FF_CHUNK):
        cols = slice(c * FF_CHUNK, (c + 1) * FF_CHUNK)
        zg = _dot(h_ext, wg_ref[:, cols])
        ext = tm + 2 * SUBLANES
        zp = pltpu.roll(zg, 1, axis=0)[SUBLANES:SUBLANES + tm]
        zn = pltpu.roll(zg, ext - 1, axis=0)[SUBLANES:SUBLANES + tm]
        conv = (zp * cw_ref[0:1, cols] + zg[SUBLANES:SUBLANES + tm] * cw_ref[1:2, cols]
                + zn * cw_ref[2:3, cols] + cb_ref[:, cols])
        act = (conv * jax.nn.sigmoid(conv) * _dot(hn, wv_ref[:, cols])).astype(BF16)
        part = _dot(act, wd_ref[cols, :])
        if c == 0:
            acc_ref[...] = part
        else:
            acc_ref[...] += part
    y = x + acc_ref[...]
    if final_norm:
        y = _rms(y, fg_ref[...])
    out_ref[0] = y


def _conv_ffn(x, g, wg, wv, cw, cb, wd, fg, tm, final_norm):
    b, s, _ = x.shape
    hb = tm // SUBLANES
    tok = pl.BlockSpec((1, tm, D_MODEL), lambda bi, i: (bi, i, 0))
    once = lambda shape: pl.BlockSpec(shape, lambda *_: (0,) * len(shape), pipeline_mode=pl.Buffered(1))
    return pl.pallas_call(
        functools.partial(_conv_ffn_kernel, final_norm=final_norm),
        grid=(b, s // tm),
        in_specs=[tok,
                  pl.BlockSpec((1, SUBLANES, D_MODEL), lambda bi, i: (bi, jnp.maximum(i * hb - 1, 0), 0)),
                  pl.BlockSpec((1, SUBLANES, D_MODEL), lambda bi, i: (bi, jnp.minimum((i + 1) * hb, s // SUBLANES - 1), 0)),
                  _const_spec((1, D_MODEL)), once((D_MODEL, D_FF_PAD)), once((D_MODEL, D_FF_PAD)),
                  _const_spec((3, D_FF_PAD)), _const_spec((1, D_FF_PAD)), once((D_FF_PAD, D_MODEL)),
                  _const_spec((1, D_MODEL))],
        out_specs=tok,
        out_shape=jax.ShapeDtypeStruct((b, s, D_MODEL), F32),
        scratch_shapes=[pltpu.VMEM((tm, D_MODEL), F32)],
        compiler_params=_params(("arbitrary", "arbitrary")),
        name="conv_ffn",
    )(x, x, x, g, wg, wv, cw, cb, wd, fg)


def _rope_tables(positions, dim, theta):
    half = dim // 2
    s = positions[0].shape[0]
    inv = theta ** (-jnp.arange(0, dim, 2, dtype=F32) / dim)
    zeros = jnp.zeros((s, half), F32)
    c, sm, sp = [], [], []
    for pos in positions:
        ang = pos.astype(F32)[:, None] * inv[None, :]
        cos, sin = jnp.cos(ang), jnp.sin(ang)
        c += [cos, cos]
        sm += [-sin, zeros]
        sp += [zeros, sin]
    rest = HEAD_DIM - dim * len(positions)
    if rest:
        c.append(jnp.ones((s, rest), F32))
        sm.append(jnp.zeros((s, rest), F32))
        sp.append(jnp.zeros((s, rest), F32))
    two = lambda parts: jnp.concatenate(parts + parts, axis=1)
    return jnp.stack([two(c), two(sm), two(sp)], axis=0)


def _block_diag_mean(width):
    blk = np.arange(width) // HEAD_DIM
    return jnp.asarray((blk[:, None] == blk[None, :]).astype(np.float32) / HEAD_DIM, BF16)


def _prep_layer(p, l):
    splits = np.cumsum((A_WIDTH, A_WIDTH, A_WIDTH, 2 * B_WIDTH, B_WIDTH, B_WIDTH, 4 * B_HEADS,
                        C_WIDTH, C_KV_WIDTH, C_KV_WIDTH))[:-1].tolist()
    a_q, a_k, a_v, b_qk, b_v, b_o, b_g, c_q, c_k, c_v = jnp.split(p["w_in"][l], splits, axis=1)
    g4 = b_g.reshape(D_MODEL, 2, 2, B_HEADS)
    gates = jnp.concatenate([g4[:, :, 0].reshape(D_MODEL, N_GATE), g4[:, :, 1].reshape(D_MODEL, N_GATE),
                             jnp.zeros((D_MODEL, LANES - 2 * N_GATE), F32)], axis=1)
    w_in = jnp.concatenate([a_q * Q_SCALE, a_k, a_v, b_qk, b_v, b_o, c_q, c_k, c_v, gates], axis=1).astype(BF16)
    gbias = jnp.concatenate([p["mlstm_igate_b"][l].reshape(N_GATE), p["mlstm_fgate_b"][l].reshape(N_GATE),
                             jnp.zeros((LANES - 2 * N_GATE,), F32)]).reshape(1, LANES)
    kscale = jnp.concatenate([jnp.ones((B_WIDTH,), F32), jnp.full((B_WIDTH,), Q_SCALE, F32)]).reshape(1, 2 * B_WIDTH)
    pad_c = lambda w: jnp.pad(w, ((0, 0), (0, D_FF_PAD - D_FF)))
    w_up = p["w_ffn_up"][l]
    return dict(
        g_mix=p["norm_mix_g"][l].reshape(1, D_MODEL), w_in=w_in,
        cw=p["mlstm_conv_w"][l], cb=p["mlstm_conv_b"][l].reshape(1, 2 * B_WIDTH), kscale=kscale, gbias=gbias,
        ng=p["mlstm_norm_g"][l].reshape(1, B_WIDTH),
        qg=jnp.tile(p["qk_norm_g"][l, 0] * Q_SCALE, C_HEADS).reshape(1, C_WIDTH),
        kg=jnp.tile(p["qk_norm_g"][l, 1], C_KV_HEADS).reshape(1, C_KV_WIDTH),
        w_out=p["w_out"][l].astype(BF16),
        g_x=p["norm_x_g"][l].reshape(1, D_MODEL), g_mem=p["norm_mem_g"][l].reshape(1, D_MODEL),
        w_xq=(p["w_xq"][l] * Q_SCALE).astype(BF16), w_xkv=p["w_xkv"][l].astype(BF16), w_xo=p["w_xo"][l].astype(BF16),
        g_ffn=p["norm_ffn_g"][l].reshape(1, D_MODEL),
        w_gate=pad_c(w_up[:, :D_FF]).astype(BF16), w_val=pad_c(w_up[:, D_FF:]).astype(BF16),
        fcw=pad_c(p["ffn_conv_w"][l]), fcb=pad_c(p["ffn_conv_b"][l].reshape(1, D_FF)),
        w_down=jnp.pad(p["w_ffn_down"][l], ((0, D_FF_PAD - D_FF), (0, 0))).astype(BF16),
    )


def _trunk(x, mem, layers, final_g, consts):
    b, s, _ = x.shape
    pos = jnp.arange(s)
    ta = _rope_tables([pos], ROPE_DIMS, ROPE_THETA)
    tc = _rope_tables([pos // GRID_W, pos % GRID_W], HEAD_DIM // 2, AXIAL_THETA)
    depth = len(layers)
    for l, w in enumerate(layers):
        kt, vm = _mem_kv(mem, w["g_mem"], w["w_xkv"])
        a, bqk, bv, bo, bg, cq, ckv = _in_proj(x, w["g_mix"], w["w_in"], w["cw"], w["cb"], w["kscale"], ta, tc,
                                               w["qg"], w["kg"], consts["bd_c"], tm=256)
        branches = [_dilated_branch(a, d) for _, d in DILATED_PAIRS]
        yb = _mlstm(bqk, bv, bo, bg, w["gbias"], w["ng"], consts["tl"], consts["tu"], consts["bd_b"])
        yc = _gqa(cq, ckv)
        x = _out_cross(x, branches, yb, yc, w["w_out"], w["g_x"], w["w_xq"], kt, vm, w["w_xo"], tm=256)
        x = _conv_ffn(x, w["g_ffn"], w["w_gate"], w["w_val"], w["fcw"], w["fcb"], w["w_down"], final_g, tm=512,
                      final_norm=(l == depth - 1))
    return x


def kernel(x_prompt, x_sample, mem_prompt, mem_sample, norm_mix_g, w_in, mlstm_conv_w, mlstm_conv_b, mlstm_igate_b, mlstm_fgate_b, mlstm_norm_g, qk_norm_g, w_out, norm_x_g, norm_mem_g, w_xq, w_xkv, w_xo, norm_ffn_g, w_ffn_up, ffn_conv_w, ffn_conv_b, w_ffn_down, final_norm_g):
    p = dict(norm_mix_g=norm_mix_g, w_in=w_in, mlstm_conv_w=mlstm_conv_w, mlstm_conv_b=mlstm_conv_b,
             mlstm_igate_b=mlstm_igate_b, mlstm_fgate_b=mlstm_fgate_b, mlstm_norm_g=mlstm_norm_g,
             qk_norm_g=qk_norm_g, w_out=w_out, norm_x_g=norm_x_g, norm_mem_g=norm_mem_g, w_xq=w_xq, w_xkv=w_xkv,
             w_xo=w_xo, norm_ffn_g=norm_ffn_g, w_ffn_up=w_ffn_up, ffn_conv_w=ffn_conv_w, ffn_conv_b=ffn_conv_b,
             w_ffn_down=w_ffn_down)
    layers = [_prep_layer(p, l) for l in range(w_in.shape[0])]
    tri = np.tril(np.ones((MLSTM_CHUNK, MLSTM_CHUNK), np.float32))
    consts = dict(bd_c=_block_diag_mean(C_WIDTH), bd_b=_block_diag_mean(B_WIDTH),
                  tl=jnp.asarray(tri, BF16), tu=jnp.asarray(tri.T, BF16))
    fg = final_norm_g.reshape(1, D_MODEL)
    return (_trunk(x_prompt, mem_prompt, layers, fg, consts), _trunk(x_sample, mem_sample, layers, fg, consts))
```

```python
import functools

import numpy as np
import jax
import jax.numpy as jnp
from jax import lax
from jax.experimental import pallas as pl
from jax.experimental.pallas import tpu as pltpu

F32 = jnp.float32
BF16 = jnp.bfloat16

LANES = 128
SUBLANES = 8
VMEM_LIMIT_BYTES = 56 * 1024 * 1024

D_MODEL = 1024
HEAD_DIM = 64
A_HEADS = 4
B_HEADS = 4
C_HEADS = 8
C_KV_HEADS = 2
A_WIDTH = A_HEADS * HEAD_DIM
B_WIDTH = B_HEADS * HEAD_DIM
C_WIDTH = C_HEADS * HEAD_DIM
C_KV_WIDTH = C_KV_HEADS * HEAD_DIM
DILATED_PAIRS = ((128, 1), (512, 4), (2048, 16))
BAND = 64
ROPE_THETA = 500000.0
ROPE_DIMS = HEAD_DIM // 4
AXIAL_THETA = 10000.0
GRID_W = 64
N_MEM = 256
X_HEADS = 4
X_WIDTH = 256
D_FF = 2752
D_FF_PAD = 2816
RMS_EPS = 1e-6
NEG_INF = -1e30
Q_SCALE = HEAD_DIM ** -0.5
LOG2_E = 1.4426950408889634

COL_A = 0
COL_BQK = COL_A + 3 * A_WIDTH
COL_BV = COL_BQK + 2 * B_WIDTH
COL_BO = COL_BV + B_WIDTH
COL_CQ = COL_BO + B_WIDTH
COL_CK = COL_CQ + C_WIDTH
COL_CV = COL_CK + C_KV_WIDTH
COL_G = COL_CV + C_KV_WIDTH
IN_COLS = COL_G + LANES
N_GATE = 2 * B_HEADS

TQ_DIL = 128
DIL_UNROLL = 4
MLSTM_CHUNK = 256
TQ_GQA = 256
KC_GQA = 1024
FF_CHUNK = 256


def _dot(a, b):
    return jnp.dot(a, b, preferred_element_type=F32)


def _dot_nt(a, b):
    return lax.dot_general(a, b, (((1,), (1,)), ((), ())), preferred_element_type=F32)


def _split2(x):
    hi = x.astype(BF16)
    return hi, (x - hi.astype(F32)).astype(BF16)


def _split3(x):
    hi = x.astype(BF16)
    r = x - hi.astype(F32)
    mid = r.astype(BF16)
    return hi, mid, (r - mid.astype(F32)).astype(BF16)


def _head_mean(xx, bd):
    hi, lo = _split2(xx)
    return _dot(hi, bd) + _dot(lo, bd)


def _rms(x, g):
    return x * lax.rsqrt(jnp.mean(x * x, axis=-1, keepdims=True) + RMS_EPS) * g


def _lane_iota(shape):
    return lax.broadcasted_iota(jnp.int32, shape, len(shape) - 1)


def _head_mask(width, h, dtype):
    lane = _lane_iota((1, width))
    return ((lane >= h * HEAD_DIM) & (lane < (h + 1) * HEAD_DIM)).astype(dtype)


def _rope(x, tab_ref, shift):
    w = x.shape[1]
    rep = w // LANES
    tile = lambda t: jnp.concatenate([t] * rep, axis=1) if rep > 1 else t
    c, sm, sp = tile(tab_ref[0]), tile(tab_ref[1]), tile(tab_ref[2])
    return x * c + pltpu.roll(x, w - shift, axis=1) * sm + pltpu.roll(x, shift, axis=1) * sp


def _shift_rows(z, prev_row, next_row):
    n = z.shape[0]
    row = lax.broadcasted_iota(jnp.int32, z.shape, 0)
    zp = jnp.where(row == 0, prev_row, pltpu.roll(z, 1, axis=0))
    zn = jnp.where(row == n - 1, next_row, pltpu.roll(z, n - 1, axis=0))
    return zp, zn


def _params(sem):
    return pltpu.CompilerParams(dimension_semantics=sem, vmem_limit_bytes=VMEM_LIMIT_BYTES)


def _const_spec(shape):
    n = len(shape)
    return pl.BlockSpec(shape, lambda *_: (0,) * n)


def _mem_kv_kernel(mem_ref, g_ref, w_ref, kt_ref, v_ref):
    hn = _rms(mem_ref[0], g_ref[...]).astype(BF16)
    kv = _dot(hn, w_ref[...])
    kt_ref[0] = kv[:, :X_WIDTH].T.astype(BF16)
    v_ref[0] = kv[:, X_WIDTH:].astype(BF16)


def _mem_kv(mem, g, w):
    b = mem.shape[0]
    return pl.pallas_call(
        _mem_kv_kernel,
        grid=(b,),
        in_specs=[pl.BlockSpec((1, N_MEM, D_MODEL), lambda i: (i, 0, 0)),
                  _const_spec((1, D_MODEL)), _const_spec((D_MODEL, 2 * X_WIDTH))],
        out_specs=[pl.BlockSpec((1, X_WIDTH, N_MEM), lambda i: (i, 0, 0)),
                   pl.BlockSpec((1, N_MEM, X_WIDTH), lambda i: (i, 0, 0))],
        out_shape=[jax.ShapeDtypeStruct((b, X_WIDTH, N_MEM), BF16),
                   jax.ShapeDtypeStruct((b, N_MEM, X_WIDTH), BF16)],
        compiler_params=_params(("arbitrary",)),
        name="mem_kv",
    )(mem, g, w)


def _in_proj_kernel(x_ref, xp_ref, xn_ref, g_ref, w_ref, cw_ref, cb_ref, ks_ref, ta_ref, tc_ref, qg_ref, kg_ref,
                    bd_ref, a1_ref, a4_ref, a16_ref, bqk_ref, bv_ref, bo_ref, bg_ref, cq_ref, ckv_ref, asc_ref):
    i = pl.program_id(1)
    last = pl.num_programs(1) - 1
    g = g_ref[...]
    hn = _rms(x_ref[0], g).astype(BF16)
    tm = hn.shape[0]

    za = _dot(hn, w_ref[:, COL_A:COL_BQK])
    av = jnp.concatenate([_rope(za[:, :2 * A_WIDTH], ta_ref, ROPE_DIMS // 2), za[:, 2 * A_WIDTH:]], axis=1)
    a1_ref[0, 0] = av.astype(BF16)
    for c in range(3 * A_WIDTH // LANES):
        asc_ref[c] = av[:, c * LANES:(c + 1) * LANES]
    for dil, ref in ((DILATED_PAIRS[1][1], a4_ref), (DILATED_PAIRS[2][1], a16_ref)):
        for r in range(dil):
            for c in range(3 * A_WIDTH // LANES):
                ref[0, r, :, c * LANES:(c + 1) * LANES] = asc_ref[c, pl.ds(r, tm // dil, stride=dil), :].astype(BF16)

    wb = w_ref[:, COL_BQK:COL_BV]
    zb = _dot(hn, wb)
    zprev = _dot(_rms(xp_ref[0], g).astype(BF16), wb)[SUBLANES - 1:SUBLANES] * (i > 0).astype(F32)
    znext = _dot(_rms(xn_ref[0], g).astype(BF16), wb)[0:1] * (i < last).astype(F32)
    zp, zn = _shift_rows(zb, zprev, znext)
    conv = zp * cw_ref[0:1] + zb * cw_ref[1:2] + zn * cw_ref[2:3] + cb_ref[...]
    bqk_ref[0] = (conv * jax.nn.sigmoid(conv) * ks_ref[...]).astype(BF16)
    zrest = _dot(hn, w_ref[:, COL_BV:COL_CQ])
    bv_ref[0] = zrest[:, :B_WIDTH].astype(BF16)
    bo_ref[0] = zrest[:, B_WIDTH:]
    bg_ref[0] = _dot(hn, w_ref[:, COL_G:IN_COLS])

    zc = _dot(hn, w_ref[:, COL_CQ:COL_G])
    cq = zc[:, :C_WIDTH]
    cq = cq * lax.rsqrt(_head_mean(cq * cq, bd_ref[...]) + RMS_EPS) * qg_ref[...]
    cq_ref[0] = _rope(cq, tc_ref, HEAD_DIM // 4).astype(BF16)
    ck = zc[:, C_WIDTH:C_WIDTH + C_KV_WIDTH]
    ck = ck * lax.rsqrt(_head_mean(ck * ck, bd_ref[:C_KV_WIDTH, :C_KV_WIDTH]) + RMS_EPS) * kg_ref[...]
    ckv_ref[0, :, :C_KV_WIDTH] = _rope(ck, tc_ref, HEAD_DIM // 4).astype(BF16)
    ckv_ref[0, :, C_KV_WIDTH:] = zc[:, C_WIDTH + C_KV_WIDTH:].astype(BF16)


def _in_proj(x, g, w, cw, cb, kscale, ta, tc, qg, kg, bd, tm):
    b, s, _ = x.shape
    nt = s // tm
    hb = tm // SUBLANES
    tok = lambda width: pl.BlockSpec((1, tm, width), lambda bi, i: (bi, i, 0))
    out_widths = (2 * B_WIDTH, B_WIDTH, B_WIDTH, LANES, C_WIDTH, 2 * C_KV_WIDTH)
    out_dtypes = (BF16, BF16, F32, F32, BF16, BF16)
    dils = [d for _, d in DILATED_PAIRS]
    a_specs = [pl.BlockSpec((1, d, tm // d, 3 * A_WIDTH), lambda bi, i: (bi, 0, i, 0)) for d in dils]
    a_shapes = [jax.ShapeDtypeStruct((b, d, s // d, 3 * A_WIDTH), BF16) for d in dils]
    return pl.pallas_call(
        _in_proj_kernel,
        grid=(b, nt),
        in_specs=[tok(D_MODEL),
                  pl.BlockSpec((1, SUBLANES, D_MODEL), lambda bi, i: (bi, jnp.maximum(i * hb - 1, 0), 0)),
                  pl.BlockSpec((1, SUBLANES, D_MODEL), lambda bi, i: (bi, jnp.minimum((i + 1) * hb, s // SUBLANES - 1), 0)),
                  _const_spec((1, D_MODEL)), _const_spec((D_MODEL, IN_COLS)),
                  _const_spec((3, 2 * B_WIDTH)), _const_spec((1, 2 * B_WIDTH)), _const_spec((1, 2 * B_WIDTH)),
                  pl.BlockSpec((3, tm, LANES), lambda bi, i: (0, i, 0)),
                  pl.BlockSpec((3, tm, LANES), lambda bi, i: (0, i, 0)),
                  _const_spec((1, C_WIDTH)), _const_spec((1, C_KV_WIDTH)), _const_spec((C_WIDTH, C_WIDTH))],
        out_specs=a_specs + [tok(wd) for wd in out_widths],
        out_shape=a_shapes + [jax.ShapeDtypeStruct((b, s, wd), dt) for wd, dt in zip(out_widths, out_dtypes)],
        scratch_shapes=[pltpu.VMEM((3 * A_WIDTH // LANES, tm, LANES), F32)],
        compiler_params=_params(("arbitrary", "arbitrary")),
        name="in_proj",
    )(x, x, x, g, w, cw, cb, kscale, ta, tc, qg, kg, bd)


def _dilated_kernel(a_ref, o_ref, l_ref, *, length, n_res, win):
    n_blk = length // TQ_DIL
    masks = [_head_mask(A_WIDTH, h, BF16) for h in range(A_HEADS)]
    lane = _lane_iota((1, A_WIDTH))
    rel = (lax.broadcasted_iota(jnp.int32, (1, TQ_DIL, win), 2)
           - lax.broadcasted_iota(jnp.int32, (1, TQ_DIL, win), 1))

    def pick(parts):
        out = parts[A_HEADS - 1]
        for h in range(A_HEADS - 2, -1, -1):
            out = jnp.where(lane < (h + 1) * HEAD_DIM, parts[h], out)
        return out

    for r in range(n_res):

        def block(jb, carry, r=r):
            if isinstance(jb, int):
                j0 = jb * TQ_DIL
                ws = min(max(j0 - BAND, 0), length - win)
            else:
                j0 = pl.multiple_of(jb * TQ_DIL, TQ_DIL)
                ws = pl.multiple_of(jnp.clip(j0 - BAND, 0, length - win), BAND)
            q = a_ref[0, r, pl.ds(j0, TQ_DIL), :A_WIDTH]
            kw = a_ref[0, r, pl.ds(ws, win), A_WIDTH:2 * A_WIDTH]
            vw = a_ref[0, r, pl.ds(ws, win), 2 * A_WIDTH:]
            q4 = jnp.concatenate([q * m for m in masks], axis=0)
            s = _dot_nt(q4, kw).reshape(A_HEADS, TQ_DIL, win)
            s = jnp.where(jnp.abs(rel + (ws - j0)) <= BAND, s, NEG_INF)
            m = jnp.max(s, axis=-1, keepdims=True)
            p = jnp.exp(s - m)
            l = jnp.sum(p, axis=-1, keepdims=True)
            pn = (p / l).astype(BF16).reshape(A_HEADS * TQ_DIL, win)
            o4 = _dot(pn, vw).reshape(A_HEADS, TQ_DIL, A_WIDTH)
            lse = m + jnp.log(l)
            o_ref[0, r, pl.ds(j0, TQ_DIL), :] = pick([o4[h] for h in range(A_HEADS)])
            l_ref[0, r, pl.ds(j0, TQ_DIL), :] = pick(
                [jnp.broadcast_to(lse[h], (TQ_DIL, A_WIDTH)) for h in range(A_HEADS)])
            return carry

        if n_blk == 1:
            block(0, 0)
        else:
            lax.fori_loop(0, n_blk, block, 0, unroll=min(n_blk, DIL_UNROLL))


def _dilated_branch(a):
    b, dilation, length, _ = a.shape
    n_res = max(1, min(dilation, 2048 // length))
    win = min(length, TQ_DIL + 2 * BAND)
    out_spec = pl.BlockSpec((1, n_res, length, A_WIDTH), lambda bi, ri: (bi, ri, 0, 0))
    return pl.pallas_call(
        functools.partial(_dilated_kernel, length=length, n_res=n_res, win=win),
        grid=(b, dilation // n_res),
        in_specs=[pl.BlockSpec((1, n_res, length, 3 * A_WIDTH), lambda bi, ri: (bi, ri, 0, 0))],
        out_specs=[out_spec, out_spec],
        out_shape=[jax.ShapeDtypeStruct((b, dilation, length, A_WIDTH), F32)] * 2,
        compiler_params=_params(("arbitrary", "arbitrary")),
        name=f"dilated_d{dilation}",
    )(a)


def _cummax_rows(x, reverse):
    n = x.shape[0]
    row = lax.broadcasted_iota(jnp.int32, x.shape, 0)
    step = 1
    while step < n:
        if reverse:
            x = jnp.maximum(x, jnp.where(row < n - step, pltpu.roll(x, n - step, axis=0), NEG_INF))
        else:
            x = jnp.maximum(x, jnp.where(row >= step, pltpu.roll(x, step, axis=0), NEG_INF))
        step *= 2
    return x


def _mlstm_kernel(qk_ref, v_ref, o_ref, g_ref, gb_ref, ng_ref, tl_ref, tu_ref, bd_ref, out_ref,
                  hb_ref, c_ref, n_ref, m_ref):
    lc = MLSTM_CHUNK
    n_chunks = qk_ref.shape[1] // lc
    lane = _lane_iota((1, B_WIDTH))
    head_masks = [_head_mask(B_WIDTH, h, BF16) for h in range(B_HEADS)]
    row_i = lax.broadcasted_iota(jnp.int32, (lc, lc), 0)
    col_j = lax.broadcasted_iota(jnp.int32, (lc, lc), 1)
    feat = lax.broadcasted_iota(jnp.int32, (B_WIDTH, B_WIDTH), 0) // HEAD_DIM
    bd_mask = feat == lax.broadcasted_iota(jnp.int32, (B_WIDTH, B_WIDTH), 1) // HEAD_DIM
    nfeat = lax.broadcasted_iota(jnp.int32, (B_WIDTH, LANES), 0) // HEAD_DIM
    nlane = lax.broadcasted_iota(jnp.int32, (B_WIDTH, LANES), 1)

    def spread(cols):
        out = cols[B_HEADS - 1]
        for h in range(B_HEADS - 2, -1, -1):
            out = jnp.where(lane < (h + 1) * HEAD_DIM, cols[h], out)
        return out

    def chunk(c, reverse):
        base = B_HEADS if reverse else 0
        r0 = pl.multiple_of(c * lc, lc)
        q = qk_ref[0, pl.ds(r0, lc), :B_WIDTH]
        k = qk_ref[0, pl.ds(r0, lc), B_WIDTH:]
        v = v_ref[0, pl.ds(r0, lc), :]
        gates = g_ref[0, pl.ds(r0, lc), :] + gb_ref[...]
        ig = gates
        lsig = jnp.minimum(gates, 0.0) - jnp.log1p(jnp.exp(-jnp.abs(gates)))
        lf = pltpu.roll(lsig, LANES - N_GATE, axis=1)
        tri = (tu_ref if reverse else tl_ref)[...]
        hi, mid, lo = _split3(lf)
        bcol = _dot(tri, hi) + _dot(tri, mid) + _dot(tri, lo)
        tot = jnp.sum(lf, axis=0, keepdims=True)
        rcol = ig - bcol
        gend = tot + rcol
        m_prev = m_ref[...]
        m_new = jnp.maximum(tot + m_prev, jnp.max(gend, axis=0, keepdims=True))
        wend = jnp.exp(gend - m_new)
        decay = jnp.exp(tot + m_prev - m_new)
        acol = -jnp.maximum(m_prev, _cummax_rows(rcol, reverse))
        wstate = jnp.exp(m_prev + acol)
        floor_ = jnp.exp(acol - bcol)
        r_t = rcol.T
        k_t = k.astype(F32).T.astype(BF16)
        q_c = _dot(q, c_ref[...].astype(BF16))
        q_n = _dot(q, n_ref[...].astype(BF16))
        mask = (col_j >= row_i) if reverse else (col_j <= row_i)
        hs = []
        for h in range(B_HEADS):
            dh = base + h
            sc = _dot(q * head_masks[h], k_t)
            e = jnp.exp(jnp.where(mask, acol[:, dh:dh + 1] + r_t[dh:dh + 1, :], NEG_INF))
            w = e * sc
            den = wstate[:, dh:dh + 1] * q_n[:, dh:dh + 1] + jnp.sum(w, axis=1, keepdims=True)
            num = wstate[:, dh:dh + 1] * q_c + _dot(w.astype(BF16), v)
            hs.append(num / jnp.maximum(jnp.abs(den), floor_[:, dh:dh + 1]))
        h_out = spread(hs)
        wend_full = spread([wend[:, base + h:base + h + 1] for h in range(B_HEADS)])
        decay_full = spread([decay[:, base + h:base + h + 1] for h in range(B_HEADS)])
        upd = _dot(k_t, (v.astype(F32) * wend_full).astype(BF16))
        c_ref[...] = decay_full * c_ref[...] + jnp.where(bd_mask, upd, 0.0)
        nupd = _dot(k_t, wend.astype(BF16))
        n_ref[...] = decay * n_ref[...] + jnp.where(nfeat == nlane - base, nupd, 0.0)
        m_ref[...] = m_new
        return r0, h_out

    def reset():
        c_ref[...] = jnp.zeros_like(c_ref)
        n_ref[...] = jnp.zeros_like(n_ref)
        m_ref[...] = jnp.zeros_like(m_ref)

    reset()

    def bwd(t, carry):
        r0, h_out = chunk(n_chunks - 1 - t, True)
        hb_ref[pl.ds(r0, lc), :] = h_out
        return carry

    lax.fori_loop(0, n_chunks, bwd, 0)
    reset()

    def fwd(t, carry):
        r0, h_out = chunk(t, False)
        hsum = h_out + hb_ref[pl.ds(r0, lc), :]
        hn = hsum * lax.rsqrt(_head_mean(hsum * hsum, bd_ref[...]) + RMS_EPS) * ng_ref[...]
        out_ref[0, pl.ds(r0, lc), :] = (jax.nn.sigmoid(o_ref[0, pl.ds(r0, lc), :]) * hn).astype(BF16)
        return carry

    lax.fori_loop(0, n_chunks, fwd, 0)


def _mlstm(bqk, bv, bo, bg, gbias, ng, tl, tu, bd):
    b, s, _ = bqk.shape
    row = lambda width: pl.BlockSpec((1, s, width), lambda i: (i, 0, 0))
    return pl.pallas_call(
        _mlstm_kernel,
        grid=(b,),
        in_specs=[row(2 * B_WIDTH), row(B_WIDTH), row(B_WIDTH), row(LANES),
                  _const_spec((1, LANES)), _const_spec((1, B_WIDTH)),
                  _const_spec((MLSTM_CHUNK, MLSTM_CHUNK)), _const_spec((MLSTM_CHUNK, MLSTM_CHUNK)),
                  _const_spec((B_WIDTH, B_WIDTH))],
        out_specs=row(B_WIDTH),
        out_shape=jax.ShapeDtypeStruct((b, s, B_WIDTH), BF16),
        scratch_shapes=[pltpu.VMEM((s, B_WIDTH), F32), pltpu.VMEM((B_WIDTH, B_WIDTH), F32),
                        pltpu.VMEM((B_WIDTH, LANES), F32), pltpu.VMEM((1, LANES), F32)],
        compiler_params=_params(("arbitrary",)),
        name="mlstm",
    )(bqk, bv, bo, bg, gbias, ng, tl, tu, bd)


def _gqa_kernel(q_ref, kv_ref, o_ref, kt_ref, vx_ref):
    s = kv_ref.shape[1]
    n_kc = s // KC_GQA
    lo = _lane_iota((1, LANES)) < HEAD_DIM
    lo_b = lo.astype(BF16)
    hi_b = 1.0 - lo_b

    @pl.when(pl.program_id(1) == 0)
    def _():
        def prep(c, carry):
            r0 = pl.multiple_of(c * KC_GQA, KC_GQA)
            kc = kv_ref[0, pl.ds(r0, KC_GQA), :C_KV_WIDTH].astype(F32)
            kr = pltpu.roll(kc, HEAD_DIM, axis=1)
            kt_ref[0, c] = jnp.where(lo, kc, kr).T.astype(BF16)
            kt_ref[1, c] = jnp.where(lo, kr, kc).T.astype(BF16)
            vc = kv_ref[0, pl.ds(r0, KC_GQA), C_KV_WIDTH:].astype(F32)
            vr = pltpu.roll(vc, HEAD_DIM, axis=1)
            vx_ref[0, pl.ds(r0, KC_GQA), :] = jnp.where(lo, vc, 0.0).astype(BF16)
            vx_ref[1, pl.ds(r0, KC_GQA), :] = jnp.where(lo, 0.0, vr).astype(BF16)
            vx_ref[2, pl.ds(r0, KC_GQA), :] = jnp.where(lo, vr, 0.0).astype(BF16)
            vx_ref[3, pl.ds(r0, KC_GQA), :] = jnp.where(lo, 0.0, vc).astype(BF16)
            return carry

        lax.fori_loop(0, n_kc, prep, 0)

    tq = q_ref.shape[1]
    per_grp = C_HEADS // C_KV_HEADS
    for grp in range(C_KV_HEADS):
        pairs = [grp * per_grp // 2 + j for j in range(per_grp // 2)]
        qs = []
        for pair in pairs:
            qp = q_ref[0, :, pair * LANES:(pair + 1) * LANES]
            qs += [qp * lo_b, qp * hi_b]

        def body(c, carry, qs=qs, grp=grp):
            ms, ls, accs = carry
            kt = kt_ref[grp, c]
            r0 = pl.multiple_of(c * KC_GQA, KC_GQA)
            new_m, new_l, alphas, ps = [], [], [], []
            for h in range(per_grp):
                sc = _dot(qs[h], kt)
                m_new = jnp.maximum(ms[h], jnp.max(sc, axis=1, keepdims=True))
                alpha = jnp.exp2(ms[h] - m_new)
                p = jnp.exp2(sc - m_new)
                new_m.append(m_new)
                new_l.append(alpha * ls[h] + jnp.sum(p, axis=1, keepdims=True))
                alphas.append(alpha)
                ps.append(p.astype(BF16))
            new_acc = []
            for j in range(per_grp // 2):
                scale = jnp.where(lo, alphas[2 * j], alphas[2 * j + 1])
                new_acc.append(scale * accs[j]
                               + _dot(ps[2 * j], vx_ref[2 * grp, pl.ds(r0, KC_GQA), :])
                               + _dot(ps[2 * j + 1], vx_ref[2 * grp + 1, pl.ds(r0, KC_GQA), :]))
            return tuple(new_m), tuple(new_l), tuple(new_acc)

        init = (tuple(jnp.full((tq, 1), NEG_INF, F32) for _ in range(per_grp)),
                tuple(jnp.zeros((tq, 1), F32) for _ in range(per_grp)),
                tuple(jnp.zeros((tq, LANES), F32) for _ in range(per_grp // 2)))
        _, ls, accs = lax.fori_loop(0, n_kc, body, init)
        for j, pair in enumerate(pairs):
            o_ref[0, :, pair * LANES:(pair + 1) * LANES] = (
                accs[j] / jnp.where(lo, ls[2 * j], ls[2 * j + 1])).astype(BF16)


def _gqa(cq, ckv):
    b, s, _ = cq.shape
    return pl.pallas_call(
        _gqa_kernel,
        grid=(b, s // TQ_GQA),
        in_specs=[pl.BlockSpec((1, TQ_GQA, C_WIDTH), lambda bi, i: (bi, i, 0)),
                  pl.BlockSpec((1, s, 2 * C_KV_WIDTH), lambda bi, i: (bi, 0, 0))],
        out_specs=pl.BlockSpec((1, TQ_GQA, C_WIDTH), lambda bi, i: (bi, i, 0)),
        out_shape=jax.ShapeDtypeStruct((b, s, C_WIDTH), BF16),
        scratch_shapes=[pltpu.VMEM((C_KV_HEADS, s // KC_GQA, LANES, KC_GQA), BF16),
                        pltpu.VMEM((2 * C_KV_HEADS, s, LANES), BF16)],
        compiler_params=_params(("arbitrary", "arbitrary")),
        name="gqa",
    )(cq, ckv)


def _out_cross_kernel(x_ref, o1_ref, l1_ref, o2_ref, l2_ref, o3_ref, l3_ref, yb_ref, yc_ref, wo_ref,
                      gx_ref, wq_ref, kt_ref, vm_ref, wxo_ref, out_ref, il_ref):
    tm = x_ref.shape[1]

    def tokens(ref, slot):
        d = ref.shape[1]
        if d == 1:
            return ref[0, 0]
        tiles = A_WIDTH // LANES
        for r in range(d):
            for c in range(tiles):
                il_ref[slot * tiles + c, pl.ds(r, tm // d, stride=d), :] = ref[0, r, :, c * LANES:(c + 1) * LANES]
        return jnp.concatenate([il_ref[slot * tiles + c] for c in range(tiles)], axis=1)

    l1, l2, l3 = tokens(l1_ref, 0), tokens(l2_ref, 0), tokens(l3_ref, 1)
    o1, o2, o3 = tokens(o1_ref, 0), tokens(o2_ref, 2), tokens(o3_ref, 3)
    mx = jnp.maximum(jnp.maximum(l1, l2), l3)
    e1, e2, e3 = jnp.exp(l1 - mx), jnp.exp(l2 - mx), jnp.exp(l3 - mx)
    ya = (e1 * o1 + e2 * o2 + e3 * o3) / (e1 + e2 + e3)
    cat = jnp.concatenate([ya.astype(BF16), yb_ref[0], yc_ref[0]], axis=1)
    x1 = x_ref[0] + _dot(cat, wo_ref[...])

    tm = x1.shape[0]
    q = _dot(_rms(x1, gx_ref[...]).astype(BF16), wq_ref[...]).astype(BF16)
    q4 = jnp.concatenate([q * _head_mask(X_WIDTH, h, BF16) for h in range(X_HEADS)], axis=0)
    s = _dot(q4, kt_ref[0])
    m = jnp.max(s, axis=1, keepdims=True)
    p = jnp.exp(s - m)
    pn = (p / jnp.sum(p, axis=1, keepdims=True)).astype(BF16)
    o4 = _dot(pn, vm_ref[0])
    lane = _lane_iota((1, X_WIDTH))
    o = o4[(X_HEADS - 1) * tm:]
    for h in range(X_HEADS - 2, -1, -1):
        o = jnp.where(lane < (h + 1) * HEAD_DIM, o4[h * tm:(h + 1) * tm], o)
    out_ref[0] = x1 + _dot(o.astype(BF16), wxo_ref[...])


def _out_cross(x, branches, yb, yc, wo, gx, wq, kt, vm, wxo, tm):
    b, s, _ = x.shape
    tok = lambda width: pl.BlockSpec((1, tm, width), lambda bi, i: (bi, i, 0))
    per_b = lambda shape: pl.BlockSpec((1,) + shape, lambda bi, i: (bi, 0, 0))
    flat = [t for pair in branches for t in pair]
    res = lambda t: pl.BlockSpec((1, t.shape[1], tm // t.shape[1], A_WIDTH), lambda bi, i: (bi, 0, i, 0))
    return pl.pallas_call(
        _out_cross_kernel,
        grid=(b, s // tm),
        in_specs=[tok(D_MODEL)] + [res(t) for t in flat] + [tok(B_WIDTH), tok(C_WIDTH),
                  _const_spec((D_MODEL, D_MODEL)), _const_spec((1, D_MODEL)), _const_spec((D_MODEL, X_WIDTH)),
                  per_b((X_WIDTH, N_MEM)), per_b((N_MEM, X_WIDTH)), _const_spec((X_WIDTH, D_MODEL))],
        out_specs=tok(D_MODEL),
        out_shape=jax.ShapeDtypeStruct((b, s, D_MODEL), F32),
        scratch_shapes=[pltpu.VMEM((4 * A_WIDTH // LANES, tm, LANES), F32)],
        compiler_params=_params(("arbitrary", "arbitrary")),
        name="out_cross",
    )(x, *flat, yb, yc, wo, gx, wq, kt, vm, wxo)


def _conv_ffn_kernel(x_ref, xp_ref, xn_ref, g_ref, wg_ref, wv_ref, cw_ref, cb_ref, wd_ref, fg_ref, out_ref, acc_ref,
                     *, final_norm):
    i = pl.program_id(1)
    last = pl.num_programs(1) - 1
    g = g_ref[...]
    x = x_ref[0]
    tm = x.shape[0]
    hn = _rms(x, g).astype(BF16)
    hp = (_rms(xp_ref[0], g) * (i > 0).astype(F32)).astype(BF16)
    hx = (_rms(xn_ref[0], g) * (i < last).astype(F32)).astype(BF16)
    h_ext = jnp.concatenate([hp, hn, hx], axis=0)
    for c in range(D_FF_PAD // FF_CHUNK):
        cols = slice(c * FF_CHUNK, (c + 1) * FF_CHUNK)
        zg = _dot(h_ext, wg_ref[:, cols])
        ext = tm + 2 * SUBLANES
        zp = pltpu.roll(zg, 1, axis=0)[SUBLANES:SUBLANES + tm]
        zn = pltpu.roll(zg, ext - 1, axis=0)[SUBLANES:SUBLANES + tm]
        conv = (zp * cw_ref[0:1, cols] + zg[SUBLANES:SUBLANES + tm] * cw_ref[1:2, cols]
                + zn * cw_ref[2:3, cols] + cb_ref[:, cols])
        act = (conv * jax.nn.sigmoid(conv) * _dot(hn, wv_ref[:, cols])).astype(BF16)
        part = _dot(act, wd_ref[cols, :])
        if c == 0:
            acc_ref[...] = part
        else:
            acc_ref[...] += part
    y = x + acc_ref[...]
    if final_norm:
        y = _rms(y, fg_ref[...])
    out_ref[0] = y


def _conv_ffn(x, g, wg, wv, cw, cb, wd, fg, tm, final_norm):
    b, s, _ = x.shape
    hb = tm // SUBLANES
    tok = pl.BlockSpec((1, tm, D_MODEL), lambda bi, i: (bi, i, 0))
    once = lambda shape: pl.BlockSpec(shape, lambda *_: (0,) * len(shape), pipeline_mode=pl.Buffered(1))
    return pl.pallas_call(
        functools.partial(_conv_ffn_kernel, final_norm=final_norm),
        grid=(b, s // tm),
        in_specs=[tok,
                  pl.BlockSpec((1, SUBLANES, D_MODEL), lambda bi, i: (bi, jnp.maximum(i * hb - 1, 0), 0)),
                  pl.BlockSpec((1, SUBLANES, D_MODEL), lambda bi, i: (bi, jnp.minimum((i + 1) * hb, s // SUBLANES - 1), 0)),
                  _const_spec((1, D_MODEL)), once((D_MODEL, D_FF_PAD)), once((D_MODEL, D_FF_PAD)),
                  _const_spec((3, D_FF_PAD)), _const_spec((1, D_FF_PAD)), once((D_FF_PAD, D_MODEL)),
                  _const_spec((1, D_MODEL))],
        out_specs=tok,
        out_shape=jax.ShapeDtypeStruct((b, s, D_MODEL), F32),
        scratch_shapes=[pltpu.VMEM((tm, D_MODEL), F32)],
        compiler_params=_params(("arbitrary", "arbitrary")),
        name="conv_ffn",
    )(x, x, x, g, wg, wv, cw, cb, wd, fg)


def _rope_tables(positions, dim, theta):
    half = dim // 2
    s = positions[0].shape[0]
    inv = theta ** (-jnp.arange(0, dim, 2, dtype=F32) / dim)
    zeros = jnp.zeros((s, half), F32)
    c, sm, sp = [], [], []
    for pos in positions:
        ang = pos.astype(F32)[:, None] * inv[None, :]
        cos, sin = jnp.cos(ang), jnp.sin(ang)
        c += [cos, cos]
        sm += [-sin, zeros]
        sp += [zeros, sin]
    rest = HEAD_DIM - dim * len(positions)
    if rest:
        c.append(jnp.ones((s, rest), F32))
        sm.append(jnp.zeros((s, rest), F32))
        sp.append(jnp.zeros((s, rest), F32))
    two = lambda parts: jnp.concatenate(parts + parts, axis=1)
    return jnp.stack([two(c), two(sm), two(sp)], axis=0)


def _block_diag_mean(width):
    blk = np.arange(width) // HEAD_DIM
    return jnp.asarray((blk[:, None] == blk[None, :]).astype(np.float32) / HEAD_DIM, BF16)


def _prep_layer(p, l):
    splits = np.cumsum((A_WIDTH, A_WIDTH, A_WIDTH, 2 * B_WIDTH, B_WIDTH, B_WIDTH, 4 * B_HEADS,
                        C_WIDTH, C_KV_WIDTH, C_KV_WIDTH))[:-1].tolist()
    a_q, a_k, a_v, b_qk, b_v, b_o, b_g, c_q, c_k, c_v = jnp.split(p["w_in"][l], splits, axis=1)
    g4 = b_g.reshape(D_MODEL, 2, 2, B_HEADS)
    gates = jnp.concatenate([g4[:, :, 0].reshape(D_MODEL, N_GATE), g4[:, :, 1].reshape(D_MODEL, N_GATE),
                             jnp.zeros((D_MODEL, LANES - 2 * N_GATE), F32)], axis=1)
    w_in = jnp.concatenate([a_q * Q_SCALE, a_k, a_v, b_qk, b_v, b_o, c_q, c_k, c_v, gates], axis=1).astype(BF16)
    gbias = jnp.concatenate([p["mlstm_igate_b"][l].reshape(N_GATE), p["mlstm_fgate_b"][l].reshape(N_GATE),
                             jnp.zeros((LANES - 2 * N_GATE,), F32)]).reshape(1, LANES)
    kscale = jnp.concatenate([jnp.ones((B_WIDTH,), F32), jnp.full((B_WIDTH,), Q_SCALE, F32)]).reshape(1, 2 * B_WIDTH)
    pad_c = lambda w: jnp.pad(w, ((0, 0), (0, D_FF_PAD - D_FF)))
    w_up = p["w_ffn_up"][l]
    return dict(
        g_mix=p["norm_mix_g"][l].reshape(1, D_MODEL), w_in=w_in,
        cw=p["mlstm_conv_w"][l], cb=p["mlstm_conv_b"][l].reshape(1, 2 * B_WIDTH), kscale=kscale, gbias=gbias,
        ng=p["mlstm_norm_g"][l].reshape(1, B_WIDTH),
        qg=jnp.tile(p["qk_norm_g"][l, 0] * (Q_SCALE * LOG2_E), C_HEADS).reshape(1, C_WIDTH),
        kg=jnp.tile(p["qk_norm_g"][l, 1], C_KV_HEADS).reshape(1, C_KV_WIDTH),
        w_out=p["w_out"][l].astype(BF16),
        g_x=p["norm_x_g"][l].reshape(1, D_MODEL), g_mem=p["norm_mem_g"][l].reshape(1, D_MODEL),
        w_xq=(p["w_xq"][l] * Q_SCALE).astype(BF16), w_xkv=p["w_xkv"][l].astype(BF16), w_xo=p["w_xo"][l].astype(BF16),
        g_ffn=p["norm_ffn_g"][l].reshape(1, D_MODEL),
        w_gate=pad_c(w_up[:, :D_FF]).astype(BF16), w_val=pad_c(w_up[:, D_FF:]).astype(BF16),
        fcw=pad_c(p["ffn_conv_w"][l]), fcb=pad_c(p["ffn_conv_b"][l].reshape(1, D_FF)),
        w_down=jnp.pad(p["w_ffn_down"][l], ((0, D_FF_PAD - D_FF), (0, 0))).astype(BF16),
    )


def _trunk(x, mem, layers, final_g, consts):
    b, s, _ = x.shape
    pos = jnp.arange(s)
    ta = _rope_tables([pos], ROPE_DIMS, ROPE_THETA)
    tc = _rope_tables([pos // GRID_W, pos % GRID_W], HEAD_DIM // 2, AXIAL_THETA)
    depth = len(layers)
    for l, w in enumerate(layers):
        kt, vm = _mem_kv(mem, w["g_mem"], w["w_xkv"])
        a1, a4, a16, bqk, bv, bo, bg, cq, ckv = _in_proj(x, w["g_mix"], w["w_in"], w["cw"], w["cb"], w["kscale"],
                                                         ta, tc, w["qg"], w["kg"], consts["bd_c"], tm=256)
        branches = [_dilated_branch(a) for a in (a1, a4, a16)]
        yb = _mlstm(bqk, bv, bo, bg, w["gbias"], w["ng"], consts["tl"], consts["tu"], consts["bd_b"])
        yc = _gqa(cq, ckv)
        x = _out_cross(x, branches, yb, yc, w["w_out"], w["g_x"], w["w_xq"], kt, vm, w["w_xo"], tm=256)
        x = _conv_ffn(x, w["g_ffn"], w["w_gate"], w["w_val"], w["fcw"], w["fcb"], w["w_down"], final_g, tm=512,
                      final_norm=(l == depth - 1))
    return x


def kernel(x_prompt, x_sample, mem_prompt, mem_sample, norm_mix_g, w_in, mlstm_conv_w, mlstm_conv_b, mlstm_igate_b, mlstm_fgate_b, mlstm_norm_g, qk_norm_g, w_out, norm_x_g, norm_mem_g, w_xq, w_xkv, w_xo, norm_ffn_g, w_ffn_up, ffn_conv_w, ffn_conv_b, w_ffn_down, final_norm_g):
    p = dict(norm_mix_g=norm_mix_g, w_in=w_in, mlstm_conv_w=mlstm_conv_w, mlstm_conv_b=mlstm_conv_b,
             mlstm_igate_b=mlstm_igate_b, mlstm_fgate_b=mlstm_fgate_b, mlstm_norm_g=mlstm_norm_g,
             qk_norm_g=qk_norm_g, w_out=w_out, norm_x_g=norm_x_g, norm_mem_g=norm_mem_g, w_xq=w_xq, w_xkv=w_xkv,
             w_xo=w_xo, norm_ffn_g=norm_ffn_g, w_ffn_up=w_ffn_up, ffn_conv_w=ffn_conv_w, ffn_conv_b=ffn_conv_b,
             w_ffn_down=w_ffn_down)
    layers = [_prep_layer(p, l) for l in range(w_in.shape[0])]
    tri = np.tril(np.ones((MLSTM_CHUNK, MLSTM_CHUNK), np.float32))
    consts = dict(bd_c=_block_diag_mean(C_WIDTH), bd_b=_block_diag_mean(B_WIDTH),
                  tl=jnp.asarray(tri, BF16), tu=jnp.asarray(tri.T, BF16))
    fg = final_norm_g.reshape(1, D_MODEL)
    return (_trunk(x_prompt, mem_prompt, layers, fg, consts), _trunk(x_sample, mem_sample, layers, fg, consts))
```

```python
import functools

import numpy as np
import jax
import jax.numpy as jnp
from jax import lax
from jax.experimental import pallas as pl
from jax.experimental.pallas import tpu as pltpu

F32 = jnp.float32
BF16 = jnp.bfloat16

LANES = 128
SUBLANES = 8
VMEM_LIMIT_BYTES = 56 * 1024 * 1024

D_MODEL = 1024
HEAD_DIM = 64
A_HEADS = 4
B_HEADS = 4
C_HEADS = 8
C_KV_HEADS = 2
A_WIDTH = A_HEADS * HEAD_DIM
B_WIDTH = B_HEADS * HEAD_DIM
C_WIDTH = C_HEADS * HEAD_DIM
C_KV_WIDTH = C_KV_HEADS * HEAD_DIM
DILATED_PAIRS = ((128, 1), (512, 4), (2048, 16))
BAND = 64
ROPE_THETA = 500000.0
ROPE_DIMS = HEAD_DIM // 4
AXIAL_THETA = 10000.0
GRID_W = 64
N_MEM = 256
X_HEADS = 4
X_WIDTH = 256
D_FF = 2752
D_FF_PAD = 2816
RMS_EPS = 1e-6
NEG_INF = -1e30
Q_SCALE = HEAD_DIM ** -0.5
LOG2_E = 1.4426950408889634

COL_A = 0
COL_BQK = COL_A + 3 * A_WIDTH
COL_BV = COL_BQK + 2 * B_WIDTH
COL_BO = COL_BV + B_WIDTH
COL_CQ = COL_BO + B_WIDTH
COL_CK = COL_CQ + C_WIDTH
COL_CV = COL_CK + C_KV_WIDTH
COL_G = COL_CV + C_KV_WIDTH
IN_COLS = COL_G + LANES
N_GATE = 2 * B_HEADS

TQ_DIL = 128
DIL_UNROLL = 8
MLSTM_CHUNK = 256
TQ_GQA = 256
KC_GQA = 1024
FF_CHUNK = 256


def _dot(a, b):
    return jnp.dot(a, b, preferred_element_type=F32)


def _dot_nt(a, b):
    return lax.dot_general(a, b, (((1,), (1,)), ((), ())), preferred_element_type=F32)


def _split2(x):
    hi = x.astype(BF16)
    return hi, (x - hi.astype(F32)).astype(BF16)


def _split3(x):
    hi = x.astype(BF16)
    r = x - hi.astype(F32)
    mid = r.astype(BF16)
    return hi, mid, (r - mid.astype(F32)).astype(BF16)


def _head_mean(xx, bd):
    hi, lo = _split2(xx)
    return _dot(hi, bd) + _dot(lo, bd)


def _rms(x, g):
    return x * lax.rsqrt(jnp.mean(x * x, axis=-1, keepdims=True) + RMS_EPS) * g


def _lane_iota(shape):
    return lax.broadcasted_iota(jnp.int32, shape, len(shape) - 1)


def _head_mask(width, h, dtype):
    lane = _lane_iota((1, width))
    return ((lane >= h * HEAD_DIM) & (lane < (h + 1) * HEAD_DIM)).astype(dtype)


def _rope(x, tab_ref, shift):
    w = x.shape[1]
    rep = w // LANES
    tile = lambda t: jnp.concatenate([t] * rep, axis=1) if rep > 1 else t
    c, sm, sp = tile(tab_ref[0]), tile(tab_ref[1]), tile(tab_ref[2])
    return x * c + pltpu.roll(x, w - shift, axis=1) * sm + pltpu.roll(x, shift, axis=1) * sp


def _shift_rows(z, prev_row, next_row):
    n = z.shape[0]
    row = lax.broadcasted_iota(jnp.int32, z.shape, 0)
    zp = jnp.where(row == 0, prev_row, pltpu.roll(z, 1, axis=0))
    zn = jnp.where(row == n - 1, next_row, pltpu.roll(z, n - 1, axis=0))
    return zp, zn


def _params(sem):
    return pltpu.CompilerParams(dimension_semantics=sem, vmem_limit_bytes=VMEM_LIMIT_BYTES)


def _const_spec(shape):
    n = len(shape)
    return pl.BlockSpec(shape, lambda *_: (0,) * n)


def _mem_kv_kernel(mem_ref, g_ref, w_ref, kt_ref, v_ref):
    hn = _rms(mem_ref[0], g_ref[...]).astype(BF16)
    kv = _dot(hn, w_ref[...])
    kt_ref[0] = kv[:, :X_WIDTH].T.astype(BF16)
    v_ref[0] = kv[:, X_WIDTH:].astype(BF16)


def _mem_kv(mem, g, w):
    b = mem.shape[0]
    return pl.pallas_call(
        _mem_kv_kernel,
        grid=(b,),
        in_specs=[pl.BlockSpec((1, N_MEM, D_MODEL), lambda i: (i, 0, 0)),
                  _const_spec((1, D_MODEL)), _const_spec((D_MODEL, 2 * X_WIDTH))],
        out_specs=[pl.BlockSpec((1, X_WIDTH, N_MEM), lambda i: (i, 0, 0)),
                   pl.BlockSpec((1, N_MEM, X_WIDTH), lambda i: (i, 0, 0))],
        out_shape=[jax.ShapeDtypeStruct((b, X_WIDTH, N_MEM), BF16),
                   jax.ShapeDtypeStruct((b, N_MEM, X_WIDTH), BF16)],
        compiler_params=_params(("arbitrary",)),
        name="mem_kv",
    )(mem, g, w)


def _in_proj_kernel(x_ref, xp_ref, xn_ref, g_ref, w_ref, cw_ref, cb_ref, ks_ref, ta_ref, tc_ref, qg_ref, kg_ref,
                    bd_ref, a1_ref, a4_ref, a16_ref, bqk_ref, bv_ref, bo_ref, bg_ref, cq_ref, ckv_ref, asc_ref):
    i = pl.program_id(1)
    last = pl.num_programs(1) - 1
    g = g_ref[...]
    hn = _rms(x_ref[0], g).astype(BF16)
    tm = hn.shape[0]

    za = _dot(hn, w_ref[:, COL_A:COL_BQK])
    av = jnp.concatenate([_rope(za[:, :2 * A_WIDTH], ta_ref, ROPE_DIMS // 2), za[:, 2 * A_WIDTH:]], axis=1)
    a1_ref[0, 0] = av.astype(BF16)
    for c in range(3 * A_WIDTH // LANES):
        asc_ref[c] = av[:, c * LANES:(c + 1) * LANES]
    for dil, ref in ((DILATED_PAIRS[1][1], a4_ref), (DILATED_PAIRS[2][1], a16_ref)):
        for r in range(dil):
            for c in range(3 * A_WIDTH // LANES):
                ref[0, r, :, c * LANES:(c + 1) * LANES] = asc_ref[c, pl.ds(r, tm // dil, stride=dil), :].astype(BF16)

    wb = w_ref[:, COL_BQK:COL_BV]
    zb = _dot(hn, wb)
    zprev = _dot(_rms(xp_ref[0], g).astype(BF16), wb)[SUBLANES - 1:SUBLANES] * (i > 0).astype(F32)
    znext = _dot(_rms(xn_ref[0], g).astype(BF16), wb)[0:1] * (i < last).astype(F32)
    zp, zn = _shift_rows(zb, zprev, znext)
    conv = zp * cw_ref[0:1] + zb * cw_ref[1:2] + zn * cw_ref[2:3] + cb_ref[...]
    bqk_ref[0] = (conv * jax.nn.sigmoid(conv) * ks_ref[...]).astype(BF16)
    zrest = _dot(hn, w_ref[:, COL_BV:COL_CQ])
    bv_ref[0] = zrest[:, :B_WIDTH].astype(BF16)
    bo_ref[0] = zrest[:, B_WIDTH:]
    bg_ref[0] = _dot(hn, w_ref[:, COL_G:IN_COLS])

    zc = _dot(hn, w_ref[:, COL_CQ:COL_G])
    cq = zc[:, :C_WIDTH]
    cq = cq * lax.rsqrt(_head_mean(cq * cq, bd_ref[...]) + RMS_EPS) * qg_ref[...]
    cq_ref[0] = _rope(cq, tc_ref, HEAD_DIM // 4).astype(BF16)
    ck = zc[:, C_WIDTH:C_WIDTH + C_KV_WIDTH]
    ck = ck * lax.rsqrt(_head_mean(ck * ck, bd_ref[:C_KV_WIDTH, :C_KV_WIDTH]) + RMS_EPS) * kg_ref[...]
    ckv_ref[0, :, :C_KV_WIDTH] = _rope(ck, tc_ref, HEAD_DIM // 4).astype(BF16)
    ckv_ref[0, :, C_KV_WIDTH:] = zc[:, C_WIDTH + C_KV_WIDTH:].astype(BF16)


def _in_proj(x, g, w, cw, cb, kscale, ta, tc, qg, kg, bd, tm):
    b, s, _ = x.shape
    nt = s // tm
    hb = tm // SUBLANES
    tok = lambda width: pl.BlockSpec((1, tm, width), lambda bi, i: (bi, i, 0))
    out_widths = (2 * B_WIDTH, B_WIDTH, B_WIDTH, LANES, C_WIDTH, 2 * C_KV_WIDTH)
    out_dtypes = (BF16, BF16, F32, F32, BF16, BF16)
    dils = [d for _, d in DILATED_PAIRS]
    a_specs = [pl.BlockSpec((1, d, tm // d, 3 * A_WIDTH), lambda bi, i: (bi, 0, i, 0)) for d in dils]
    a_shapes = [jax.ShapeDtypeStruct((b, d, s // d, 3 * A_WIDTH), BF16) for d in dils]
    return pl.pallas_call(
        _in_proj_kernel,
        grid=(b, nt),
        in_specs=[tok(D_MODEL),
                  pl.BlockSpec((1, SUBLANES, D_MODEL), lambda bi, i: (bi, jnp.maximum(i * hb - 1, 0), 0)),
                  pl.BlockSpec((1, SUBLANES, D_MODEL), lambda bi, i: (bi, jnp.minimum((i + 1) * hb, s // SUBLANES - 1), 0)),
                  _const_spec((1, D_MODEL)), _const_spec((D_MODEL, IN_COLS)),
                  _const_spec((3, 2 * B_WIDTH)), _const_spec((1, 2 * B_WIDTH)), _const_spec((1, 2 * B_WIDTH)),
                  pl.BlockSpec((3, tm, LANES), lambda bi, i: (0, i, 0)),
                  pl.BlockSpec((3, tm, LANES), lambda bi, i: (0, i, 0)),
                  _const_spec((1, C_WIDTH)), _const_spec((1, C_KV_WIDTH)), _const_spec((C_WIDTH, C_WIDTH))],
        out_specs=a_specs + [tok(wd) for wd in out_widths],
        out_shape=a_shapes + [jax.ShapeDtypeStruct((b, s, wd), dt) for wd, dt in zip(out_widths, out_dtypes)],
        scratch_shapes=[pltpu.VMEM((3 * A_WIDTH // LANES, tm, LANES), F32)],
        compiler_params=_params(("arbitrary", "arbitrary")),
        name="in_proj",
    )(x, x, x, g, w, cw, cb, kscale, ta, tc, qg, kg, bd)


def _dilated_kernel(a_ref, o_ref, l_ref, *, length, n_res, win):
    n_blk = length // TQ_DIL
    masks = [_head_mask(A_WIDTH, h, BF16) for h in range(A_HEADS)]
    lane = _lane_iota((1, A_WIDTH))
    rel = (lax.broadcasted_iota(jnp.int32, (1, TQ_DIL, win), 2)
           - lax.broadcasted_iota(jnp.int32, (1, TQ_DIL, win), 1))

    def pick(parts):
        out = parts[A_HEADS - 1]
        for h in range(A_HEADS - 2, -1, -1):
            out = jnp.where(lane < (h + 1) * HEAD_DIM, parts[h], out)
        return out

    for r in range(n_res):

        def block(jb, carry, r=r):
            if isinstance(jb, int):
                j0 = jb * TQ_DIL
                ws = min(max(j0 - BAND, 0), length - win)
            else:
                j0 = pl.multiple_of(jb * TQ_DIL, TQ_DIL)
                ws = pl.multiple_of(jnp.clip(j0 - BAND, 0, length - win), BAND)
            q = a_ref[0, r, pl.ds(j0, TQ_DIL), :A_WIDTH]
            kw = a_ref[0, r, pl.ds(ws, win), A_WIDTH:2 * A_WIDTH]
            vw = a_ref[0, r, pl.ds(ws, win), 2 * A_WIDTH:]
            q4 = jnp.concatenate([q * m for m in masks], axis=0)
            s = _dot_nt(q4, kw).reshape(A_HEADS, TQ_DIL, win)
            s = jnp.where(jnp.abs(rel + (ws - j0)) <= BAND, s, NEG_INF)
            m = jnp.max(s, axis=-1, keepdims=True)
            p = jnp.exp(s - m)
            l = jnp.sum(p, axis=-1, keepdims=True)
            pn = (p / l).astype(BF16).reshape(A_HEADS * TQ_DIL, win)
            o4 = _dot(pn, vw).reshape(A_HEADS, TQ_DIL, A_WIDTH)
            lse = m + jnp.log(l)
            o_ref[0, r, pl.ds(j0, TQ_DIL), :] = pick([o4[h] for h in range(A_HEADS)])
            l_ref[0, r, pl.ds(j0, TQ_DIL), :] = pick(
                [jnp.broadcast_to(lse[h], (TQ_DIL, A_WIDTH)) for h in range(A_HEADS)])
            return carry

        if n_blk == 1:
            block(0, 0)
        else:
            lax.fori_loop(0, n_blk, block, 0, unroll=min(n_blk, DIL_UNROLL))


def _dilated_branch(a):
    b, dilation, length, _ = a.shape
    n_res = max(1, min(dilation, 2048 // length))
    win = min(length, TQ_DIL + 2 * BAND)
    out_spec = pl.BlockSpec((1, n_res, length, A_WIDTH), lambda bi, ri: (bi, ri, 0, 0))
    return pl.pallas_call(
        functools.partial(_dilated_kernel, length=length, n_res=n_res, win=win),
        grid=(b, dilation // n_res),
        in_specs=[pl.BlockSpec((1, n_res, length, 3 * A_WIDTH), lambda bi, ri: (bi, ri, 0, 0))],
        out_specs=[out_spec, out_spec],
        out_shape=[jax.ShapeDtypeStruct((b, dilation, length, A_WIDTH), F32)] * 2,
        compiler_params=_params(("arbitrary", "arbitrary")),
        name=f"dilated_d{dilation}",
    )(a)


def _cummax_rows(x, reverse):
    n = x.shape[0]
    row = lax.broadcasted_iota(jnp.int32, x.shape, 0)
    step = 1
    while step < n:
        if reverse:
            x = jnp.maximum(x, jnp.where(row < n - step, pltpu.roll(x, n - step, axis=0), NEG_INF))
        else:
            x = jnp.maximum(x, jnp.where(row >= step, pltpu.roll(x, step, axis=0), NEG_INF))
        step *= 2
    return x


def _mlstm_kernel(qk_ref, v_ref, o_ref, g_ref, gb_ref, ng_ref, tl_ref, tu_ref, bd_ref, out_ref,
                  hbuf_ref, c_ref, n_ref, m_ref):
    lc = MLSTM_CHUNK
    n_chunks = qk_ref.shape[1] // lc
    lane = _lane_iota((1, B_WIDTH))
    glane = _lane_iota((1, LANES))
    head_masks = [_head_mask(B_WIDTH, h, BF16) for h in range(B_HEADS)]
    row_i = lax.broadcasted_iota(jnp.int32, (lc, lc), 0)
    col_j = lax.broadcasted_iota(jnp.int32, (lc, lc), 1)
    feat = lax.broadcasted_iota(jnp.int32, (B_WIDTH, B_WIDTH), 0) // HEAD_DIM
    bd_mask = feat == lax.broadcasted_iota(jnp.int32, (B_WIDTH, B_WIDTH), 1) // HEAD_DIM
    nfeat = lax.broadcasted_iota(jnp.int32, (B_WIDTH, LANES), 0) // HEAD_DIM
    nlane = lax.broadcasted_iota(jnp.int32, (B_WIDTH, LANES), 1)

    def spread(cols):
        out = cols[B_HEADS - 1]
        for h in range(B_HEADS - 2, -1, -1):
            out = jnp.where(lane < (h + 1) * HEAD_DIM, cols[h], out)
        return out

    def chunk(c, reverse):
        base = B_HEADS if reverse else 0
        r0 = pl.multiple_of(c * lc, lc)
        q = qk_ref[0, pl.ds(r0, lc), :B_WIDTH]
        k = qk_ref[0, pl.ds(r0, lc), B_WIDTH:]
        v = v_ref[0, pl.ds(r0, lc), :]
        gates = g_ref[0, pl.ds(r0, lc), :] + gb_ref[...]
        ig = gates
        lsig = jnp.minimum(gates, 0.0) - jnp.log1p(jnp.exp(-jnp.abs(gates)))
        lf = pltpu.roll(lsig, LANES - N_GATE, axis=1)
        tri = (tu_ref if reverse else tl_ref)[...]
        hi, mid, lo = _split3(lf)
        bcol = _dot(tri, hi) + _dot(tri, mid) + _dot(tri, lo)
        tot = jnp.sum(lf, axis=0, keepdims=True)
        rcol = ig - bcol
        gend = tot + rcol
        d = int(reverse)
        m_prev = m_ref[d]
        m_new = jnp.maximum(tot + m_prev, jnp.max(gend, axis=0, keepdims=True))
        wend = jnp.exp(gend - m_new)
        decay = jnp.exp(tot + m_prev - m_new)
        acol = -jnp.maximum(m_prev, _cummax_rows(rcol, reverse))
        wstate = jnp.exp(m_prev + acol)
        floor_ = jnp.exp(acol - bcol)
        a2 = acol * LOG2_E
        r2_t = (rcol * LOG2_E).T
        k_t = k.astype(F32).T.astype(BF16)
        q_c = _dot(q, c_ref[d].astype(BF16))
        q_n = _dot(q, n_ref[d].astype(BF16))
        mask = (col_j >= row_i) if reverse else (col_j <= row_i)
        ws, sum_w = [], jnp.zeros((lc, LANES), F32)
        for h in range(B_HEADS):
            dh = base + h
            sc = _dot(q * head_masks[h], k_t)
            e = jnp.exp2(jnp.where(mask, a2[:, dh:dh + 1] + r2_t[dh:dh + 1, :], NEG_INF))
            w = e * sc
            sum_w = jnp.where(glane == dh, jnp.sum(w, axis=1, keepdims=True), sum_w)
            ws.append(w.astype(BF16))
        den = wstate * q_n + sum_w
        rinv = 1.0 / jnp.maximum(jnp.abs(den), floor_)
        heads = lambda t: spread([t[:, base + h:base + h + 1] for h in range(B_HEADS)])
        pv = _dot(jnp.concatenate(ws, axis=1), jnp.concatenate([v * hm for hm in head_masks], axis=0))
        h_out = (heads(wstate) * q_c + pv) * heads(rinv)
        upd = _dot(k_t, (v.astype(F32) * heads(wend)).astype(BF16))
        c_ref[d] = heads(decay) * c_ref[d] + jnp.where(bd_mask, upd, 0.0)
        nupd = _dot(k_t, wend.astype(BF16))
        n_ref[d] = decay * n_ref[d] + jnp.where(nfeat == nlane - base, nupd, 0.0)
        m_ref[d] = m_new
        return r0, h_out

    def finish(r0, hsum):
        hn = hsum * lax.rsqrt(_head_mean(hsum * hsum, bd_ref[...]) + RMS_EPS) * ng_ref[...]
        out_ref[0, pl.ds(r0, lc), :] = (jax.nn.sigmoid(o_ref[0, pl.ds(r0, lc), :]) * hn).astype(BF16)

    c_ref[...] = jnp.zeros_like(c_ref)
    n_ref[...] = jnp.zeros_like(n_ref)
    m_ref[...] = jnp.zeros_like(m_ref)

    def first_half(t, carry):
        for reverse in (False, True):
            r0, h_out = chunk(n_chunks - 1 - t if reverse else t, reverse)
            hbuf_ref[pl.ds(r0, lc), :] = h_out
        return carry

    def second_half(t, carry):
        for reverse in (False, True):
            r0, h_out = chunk(n_chunks - 1 - t if reverse else t, reverse)
            finish(r0, h_out + hbuf_ref[pl.ds(r0, lc), :])
        return carry

    lax.fori_loop(0, n_chunks // 2, first_half, 0)
    lax.fori_loop(n_chunks // 2, n_chunks, second_half, 0)


def _mlstm(bqk, bv, bo, bg, gbias, ng, tl, tu, bd):
    b, s, _ = bqk.shape
    assert s % (2 * MLSTM_CHUNK) == 0
    row = lambda width: pl.BlockSpec((1, s, width), lambda i: (i, 0, 0))
    return pl.pallas_call(
        _mlstm_kernel,
        grid=(b,),
        in_specs=[row(2 * B_WIDTH), row(B_WIDTH), row(B_WIDTH), row(LANES),
                  _const_spec((1, LANES)), _const_spec((1, B_WIDTH)),
                  _const_spec((MLSTM_CHUNK, MLSTM_CHUNK)), _const_spec((MLSTM_CHUNK, MLSTM_CHUNK)),
                  _const_spec((B_WIDTH, B_WIDTH))],
        out_specs=row(B_WIDTH),
        out_shape=jax.ShapeDtypeStruct((b, s, B_WIDTH), BF16),
        scratch_shapes=[pltpu.VMEM((s, B_WIDTH), F32), pltpu.VMEM((2, B_WIDTH, B_WIDTH), F32),
                        pltpu.VMEM((2, B_WIDTH, LANES), F32), pltpu.VMEM((2, 1, LANES), F32)],
        compiler_params=_params(("arbitrary",)),
        name="mlstm",
    )(bqk, bv, bo, bg, gbias, ng, tl, tu, bd)


def _gqa_kernel(q_ref, kv_ref, o_ref, kt_ref, vx_ref):
    s = kv_ref.shape[1]
    n_kc = s // KC_GQA
    lo = _lane_iota((1, LANES)) < HEAD_DIM
    lo_b = lo.astype(BF16)
    hi_b = 1.0 - lo_b

    @pl.when(pl.program_id(1) == 0)
    def _():
        def prep(c, carry):
            r0 = pl.multiple_of(c * KC_GQA, KC_GQA)
            kc = kv_ref[0, pl.ds(r0, KC_GQA), :C_KV_WIDTH].astype(F32)
            kr = pltpu.roll(kc, HEAD_DIM, axis=1)
            kt_ref[0, c] = jnp.where(lo, kc, kr).T.astype(BF16)
            kt_ref[1, c] = jnp.where(lo, kr, kc).T.astype(BF16)
            vc = kv_ref[0, pl.ds(r0, KC_GQA), C_KV_WIDTH:].astype(F32)
            vr = pltpu.roll(vc, HEAD_DIM, axis=1)
            vx_ref[0, pl.ds(r0, KC_GQA), :] = jnp.where(lo, vc, 0.0).astype(BF16)
            vx_ref[1, pl.ds(r0, KC_GQA), :] = jnp.where(lo, 0.0, vr).astype(BF16)
            vx_ref[2, pl.ds(r0, KC_GQA), :] = jnp.where(lo, vr, 0.0).astype(BF16)
            vx_ref[3, pl.ds(r0, KC_GQA), :] = jnp.where(lo, 0.0, vc).astype(BF16)
            return carry

        lax.fori_loop(0, n_kc, prep, 0)

    tq = q_ref.shape[1]
    per_grp = C_HEADS // C_KV_HEADS
    for grp in range(C_KV_HEADS):
        pairs = [grp * per_grp // 2 + j for j in range(per_grp // 2)]
        qs = []
        for pair in pairs:
            qp = q_ref[0, :, pair * LANES:(pair + 1) * LANES]
            qs += [qp * lo_b, qp * hi_b]

        def body(c, carry, qs=qs, grp=grp):
            ms, ls, accs = carry
            kt = kt_ref[grp, c]
            r0 = pl.multiple_of(c * KC_GQA, KC_GQA)
            new_m, new_l, alphas, ps = [], [], [], []
            for h in range(per_grp):
                sc = _dot(qs[h], kt)
                m_new = jnp.maximum(ms[h], jnp.max(sc, axis=1, keepdims=True))
                alpha = jnp.exp2(ms[h] - m_new)
                p = jnp.exp2(sc - m_new)
                new_m.append(m_new)
                new_l.append(alpha * ls[h] + jnp.sum(p, axis=1, keepdims=True))
                alphas.append(alpha)
                ps.append(p.astype(BF16))
            new_acc = []
            for j in range(per_grp // 2):
                scale = jnp.where(lo, alphas[2 * j], alphas[2 * j + 1])
                new_acc.append(scale * accs[j]
                               + _dot(ps[2 * j], vx_ref[2 * grp, pl.ds(r0, KC_GQA), :])
                               + _dot(ps[2 * j + 1], vx_ref[2 * grp + 1, pl.ds(r0, KC_GQA), :]))
            return tuple(new_m), tuple(new_l), tuple(new_acc)

        init = (tuple(jnp.full((tq, 1), NEG_INF, F32) for _ in range(per_grp)),
                tuple(jnp.zeros((tq, 1), F32) for _ in range(per_grp)),
                tuple(jnp.zeros((tq, LANES), F32) for _ in range(per_grp // 2)))
        _, ls, accs = lax.fori_loop(0, n_kc, body, init, unroll=True)
        for j, pair in enumerate(pairs):
            o_ref[0, :, pair * LANES:(pair + 1) * LANES] = (
                accs[j] / jnp.where(lo, ls[2 * j], ls[2 * j + 1])).astype(BF16)


def _gqa(cq, ckv):
    b, s, _ = cq.shape
    return pl.pallas_call(
        _gqa_kernel,
        grid=(b, s // TQ_GQA),
        in_specs=[pl.BlockSpec((1, TQ_GQA, C_WIDTH), lambda bi, i: (bi, i, 0)),
                  pl.BlockSpec((1, s, 2 * C_KV_WIDTH), lambda bi, i: (bi, 0, 0))],
        out_specs=pl.BlockSpec((1, TQ_GQA, C_WIDTH), lambda bi, i: (bi, i, 0)),
        out_shape=jax.ShapeDtypeStruct((b, s, C_WIDTH), BF16),
        scratch_shapes=[pltpu.VMEM((C_KV_HEADS, s // KC_GQA, LANES, KC_GQA), BF16),
                        pltpu.VMEM((2 * C_KV_HEADS, s, LANES), BF16)],
        compiler_params=_params(("arbitrary", "arbitrary")),
        name="gqa",
    )(cq, ckv)


def _out_cross_kernel(x_ref, o1_ref, l1_ref, o2_ref, l2_ref, o3_ref, l3_ref, yb_ref, yc_ref, wo_ref,
                      gx_ref, wq_ref, kt_ref, vm_ref, wxo_ref, out_ref, il_ref):
    tm = x_ref.shape[1]

    def tokens(ref, slot):
        d = ref.shape[1]
        if d == 1:
            return ref[0, 0]
        tiles = A_WIDTH // LANES
        for r in range(d):
            for c in range(tiles):
                il_ref[slot * tiles + c, pl.ds(r, tm // d, stride=d), :] = ref[0, r, :, c * LANES:(c + 1) * LANES]
        return jnp.concatenate([il_ref[slot * tiles + c] for c in range(tiles)], axis=1)

    l1, l2, l3 = tokens(l1_ref, 0), tokens(l2_ref, 0), tokens(l3_ref, 1)
    o1, o2, o3 = tokens(o1_ref, 0), tokens(o2_ref, 2), tokens(o3_ref, 3)
    mx = jnp.maximum(jnp.maximum(l1, l2), l3)
    e1, e2, e3 = jnp.exp(l1 - mx), jnp.exp(l2 - mx), jnp.exp(l3 - mx)
    ya = (e1 * o1 + e2 * o2 + e3 * o3) / (e1 + e2 + e3)
    cat = jnp.concatenate([ya.astype(BF16), yb_ref[0], yc_ref[0]], axis=1)
    x1 = x_ref[0] + _dot(cat, wo_ref[...])

    tm = x1.shape[0]
    q = _dot(_rms(x1, gx_ref[...]).astype(BF16), wq_ref[...]).astype(BF16)
    q4 = jnp.concatenate([q * _head_mask(X_WIDTH, h, BF16) for h in range(X_HEADS)], axis=0)
    s = _dot(q4, kt_ref[0])
    m = jnp.max(s, axis=1, keepdims=True)
    p = jnp.exp(s - m)
    pn = (p / jnp.sum(p, axis=1, keepdims=True)).astype(BF16)
    o4 = _dot(pn, vm_ref[0])
    lane = _lane_iota((1, X_WIDTH))
    o = o4[(X_HEADS - 1) * tm:]
    for h in range(X_HEADS - 2, -1, -1):
        o = jnp.where(lane < (h + 1) * HEAD_DIM, o4[h * tm:(h + 1) * tm], o)
    out_ref[0] = x1 + _dot(o.astype(BF16), wxo_ref[...])


def _out_cross(x, branches, yb, yc, wo, gx, wq, kt, vm, wxo, tm):
    b, s, _ = x.shape
    tok = lambda width: pl.BlockSpec((1, tm, width), lambda bi, i: (bi, i, 0))
    per_b = lambda shape: pl.BlockSpec((1,) + shape, lambda bi, i: (bi, 0, 0))
    flat = [t for pair in branches for t in pair]
    res = lambda t: pl.BlockSpec((1, t.shape[1], tm // t.shape[1], A_WIDTH), lambda bi, i: (bi, 0, i, 0))
    return pl.pallas_call(
        _out_cross_kernel,
        grid=(b, s // tm),
        in_specs=[tok(D_MODEL)] + [res(t) for t in flat] + [tok(B_WIDTH), tok(C_WIDTH),
                  _const_spec((D_MODEL, D_MODEL)), _const_spec((1, D_MODEL)), _const_spec((D_MODEL, X_WIDTH)),
                  per_b((X_WIDTH, N_MEM)), per_b((N_MEM, X_WIDTH)), _const_spec((X_WIDTH, D_MODEL))],
        out_specs=tok(D_MODEL),
        out_shape=jax.ShapeDtypeStruct((b, s, D_MODEL), F32),
        scratch_shapes=[pltpu.VMEM((4 * A_WIDTH // LANES, tm, LANES), F32)],
        compiler_params=_params(("arbitrary", "arbitrary")),
        name="out_cross",
    )(x, *flat, yb, yc, wo, gx, wq, kt, vm, wxo)


def _conv_ffn_kernel(x_ref, xp_ref, xn_ref, g_ref, wg_ref, wv_ref, cw_ref, cb_ref, wd_ref, fg_ref, out_ref, act_ref,
                     *, final_norm):
    i = pl.program_id(1)
    last = pl.num_programs(1) - 1
    g = g_ref[...]
    x = x_ref[0]
    tm = x.shape[0]
    hn = _rms(x, g).astype(BF16)
    hp = (_rms(xp_ref[0], g) * (i > 0).astype(F32)).astype(BF16)
    hx = (_rms(xn_ref[0], g) * (i < last).astype(F32)).astype(BF16)
    h_ext = jnp.concatenate([hp, hn, hx], axis=0)
    for c in range(D_FF_PAD // FF_CHUNK):
        cols = slice(c * FF_CHUNK, (c + 1) * FF_CHUNK)
        zg = _dot(h_ext, wg_ref[:, cols])
        ext = tm + 2 * SUBLANES
        zp = pltpu.roll(zg, 1, axis=0)[SUBLANES:SUBLANES + tm]
        zn = pltpu.roll(zg, ext - 1, axis=0)[SUBLANES:SUBLANES + tm]
        conv = (zp * cw_ref[0:1, cols] + zg[SUBLANES:SUBLANES + tm] * cw_ref[1:2, cols]
                + zn * cw_ref[2:3, cols] + cb_ref[:, cols])
        act_ref[:, cols] = (conv * jax.nn.sigmoid(conv) * _dot(hn, wv_ref[:, cols])).astype(BF16)
    y = x + _dot(act_ref[...], wd_ref[...])
    if final_norm:
        y = _rms(y, fg_ref[...])
    out_ref[0] = y


def _conv_ffn(x, g, wg, wv, cw, cb, wd, fg, tm, final_norm):
    b, s, _ = x.shape
    hb = tm // SUBLANES
    tok = pl.BlockSpec((1, tm, D_MODEL), lambda bi, i: (bi, i, 0))
    once = lambda shape: pl.BlockSpec(shape, lambda *_: (0,) * len(shape), pipeline_mode=pl.Buffered(1))
    return pl.pallas_call(
        functools.partial(_conv_ffn_kernel, final_norm=final_norm),
        grid=(b, s // tm),
        in_specs=[tok,
                  pl.BlockSpec((1, SUBLANES, D_MODEL), lambda bi, i: (bi, jnp.maximum(i * hb - 1, 0), 0)),
                  pl.BlockSpec((1, SUBLANES, D_MODEL), lambda bi, i: (bi, jnp.minimum((i + 1) * hb, s // SUBLANES - 1), 0)),
                  _const_spec((1, D_MODEL)), once((D_MODEL, D_FF_PAD)), once((D_MODEL, D_FF_PAD)),
                  _const_spec((3, D_FF_PAD)), _const_spec((1, D_FF_PAD)), once((D_FF_PAD, D_MODEL)),
                  _const_spec((1, D_MODEL))],
        out_specs=tok,
        out_shape=jax.ShapeDtypeStruct((b, s, D_MODEL), F32),
        scratch_shapes=[pltpu.VMEM((tm, D_FF_PAD), BF16)],
        compiler_params=_params(("arbitrary", "arbitrary")),
        name="conv_ffn",
    )(x, x, x, g, wg, wv, cw, cb, wd, fg)


def _rope_tables(positions, dim, theta):
    half = dim // 2
    s = positions[0].shape[0]
    inv = theta ** (-jnp.arange(0, dim, 2, dtype=F32) / dim)
    zeros = jnp.zeros((s, half), F32)
    c, sm, sp = [], [], []
    for pos in positions:
        ang = pos.astype(F32)[:, None] * inv[None, :]
        cos, sin = jnp.cos(ang), jnp.sin(ang)
        c += [cos, cos]
        sm += [-sin, zeros]
        sp += [zeros, sin]
    rest = HEAD_DIM - dim * len(positions)
    if rest:
        c.append(jnp.ones((s, rest), F32))
        sm.append(jnp.zeros((s, rest), F32))
        sp.append(jnp.zeros((s, rest), F32))
    two = lambda parts: jnp.concatenate(parts + parts, axis=1)
    return jnp.stack([two(c), two(sm), two(sp)], axis=0)


def _block_diag_mean(width):
    blk = np.arange(width) // HEAD_DIM
    return jnp.asarray((blk[:, None] == blk[None, :]).astype(np.float32) / HEAD_DIM, BF16)


def _prep_layer(p, l):
    splits = np.cumsum((A_WIDTH, A_WIDTH, A_WIDTH, 2 * B_WIDTH, B_WIDTH, B_WIDTH, 4 * B_HEADS,
                        C_WIDTH, C_KV_WIDTH, C_KV_WIDTH))[:-1].tolist()
    a_q, a_k, a_v, b_qk, b_v, b_o, b_g, c_q, c_k, c_v = jnp.split(p["w_in"][l], splits, axis=1)
    g4 = b_g.reshape(D_MODEL, 2, 2, B_HEADS)
    gates = jnp.concatenate([g4[:, :, 0].reshape(D_MODEL, N_GATE), g4[:, :, 1].reshape(D_MODEL, N_GATE),
                             jnp.zeros((D_MODEL, LANES - 2 * N_GATE), F32)], axis=1)
    w_in = jnp.concatenate([a_q * Q_SCALE, a_k, a_v, b_qk, b_v, b_o, c_q, c_k, c_v, gates], axis=1).astype(BF16)
    gbias = jnp.concatenate([p["mlstm_igate_b"][l].reshape(N_GATE), p["mlstm_fgate_b"][l].reshape(N_GATE),
                             jnp.zeros((LANES - 2 * N_GATE,), F32)]).reshape(1, LANES)
    kscale = jnp.concatenate([jnp.ones((B_WIDTH,), F32), jnp.full((B_WIDTH,), Q_SCALE, F32)]).reshape(1, 2 * B_WIDTH)
    pad_c = lambda w: jnp.pad(w, ((0, 0), (0, D_FF_PAD - D_FF)))
    w_up = p["w_ffn_up"][l]
    return dict(
        g_mix=p["norm_mix_g"][l].reshape(1, D_MODEL), w_in=w_in,
        cw=p["mlstm_conv_w"][l], cb=p["mlstm_conv_b"][l].reshape(1, 2 * B_WIDTH), kscale=kscale, gbias=gbias,
        ng=p["mlstm_norm_g"][l].reshape(1, B_WIDTH),
        qg=jnp.tile(p["qk_norm_g"][l, 0] * (Q_SCALE * LOG2_E), C_HEADS).reshape(1, C_WIDTH),
        kg=jnp.tile(p["qk_norm_g"][l, 1], C_KV_HEADS).reshape(1, C_KV_WIDTH),
        w_out=p["w_out"][l].astype(BF16),
        g_x=p["norm_x_g"][l].reshape(1, D_MODEL), g_mem=p["norm_mem_g"][l].reshape(1, D_MODEL),
        w_xq=(p["w_xq"][l] * Q_SCALE).astype(BF16), w_xkv=p["w_xkv"][l].astype(BF16), w_xo=p["w_xo"][l].astype(BF16),
        g_ffn=p["norm_ffn_g"][l].reshape(1, D_MODEL),
        w_gate=pad_c(w_up[:, :D_FF]).astype(BF16), w_val=pad_c(w_up[:, D_FF:]).astype(BF16),
        fcw=pad_c(p["ffn_conv_w"][l]), fcb=pad_c(p["ffn_conv_b"][l].reshape(1, D_FF)),
        w_down=jnp.pad(p["w_ffn_down"][l], ((0, D_FF_PAD - D_FF), (0, 0))).astype(BF16),
    )


def _trunk(x, mem, layers, final_g, consts):
    b, s, _ = x.shape
    pos = jnp.arange(s)
    ta = _rope_tables([pos], ROPE_DIMS, ROPE_THETA)
    tc = _rope_tables([pos // GRID_W, pos % GRID_W], HEAD_DIM // 2, AXIAL_THETA)
    depth = len(layers)
    for l, w in enumerate(layers):
        kt, vm = _mem_kv(mem, w["g_mem"], w["w_xkv"])
        a1, a4, a16, bqk, bv, bo, bg, cq, ckv = _in_proj(x, w["g_mix"], w["w_in"], w["cw"], w["cb"], w["kscale"],
                                                         ta, tc, w["qg"], w["kg"], consts["bd_c"], tm=512)
        branches = [_dilated_branch(a) for a in (a1, a4, a16)]
        yb = _mlstm(bqk, bv, bo, bg, w["gbias"], w["ng"], consts["tl"], consts["tu"], consts["bd_b"])
        yc = _gqa(cq, ckv)
        x = _out_cross(x, branches, yb, yc, w["w_out"], w["g_x"], w["w_xq"], kt, vm, w["w_xo"], tm=512)
        x = _conv_ffn(x, w["g_ffn"], w["w_gate"], w["w_val"], w["fcw"], w["fcb"], w["w_down"], final_g, tm=512,
                      final_norm=(l == depth - 1))
    return x


def kernel(x_prompt, x_sample, mem_prompt, mem_sample, norm_mix_g, w_in, mlstm_conv_w, mlstm_conv_b, mlstm_igate_b, mlstm_fgate_b, mlstm_norm_g, qk_norm_g, w_out, norm_x_g, norm_mem_g, w_xq, w_xkv, w_xo, norm_ffn_g, w_ffn_up, ffn_conv_w, ffn_conv_b, w_ffn_down, final_norm_g):
    p = dict(norm_mix_g=norm_mix_g, w_in=w_in, mlstm_conv_w=mlstm_conv_w, mlstm_conv_b=mlstm_conv_b,
             mlstm_igate_b=mlstm_igate_b, mlstm_fgate_b=mlstm_fgate_b, mlstm_norm_g=mlstm_norm_g,
             qk_norm_g=qk_norm_g, w_out=w_out, norm_x_g=norm_x_g, norm_mem_g=norm_mem_g, w_xq=w_xq, w_xkv=w_xkv,
             w_xo=w_xo, norm_ffn_g=norm_ffn_g, w_ffn_up=w_ffn_up, ffn_conv_w=ffn_conv_w, ffn_conv_b=ffn_conv_b,
             w_ffn_down=w_ffn_down)
    layers = [_prep_layer(p, l) for l in range(w_in.shape[0])]
    tri = np.tril(np.ones((MLSTM_CHUNK, MLSTM_CHUNK), np.float32))
    consts = dict(bd_c=_block_diag_mean(C_WIDTH), bd_b=_block_diag_mean(B_WIDTH),
                  tl=jnp.asarray(tri, BF16), tu=jnp.asarray(tri.T, BF16))
    fg = final_norm_g.reshape(1, D_MODEL)
    return (_trunk(x_prompt, mem_prompt, layers, fg, consts), _trunk(x_sample, mem_sample, layers, fg, consts))
```

```python
import functools

import numpy as np
import jax
import jax.numpy as jnp
from jax import lax
from jax.experimental import pallas as pl
from jax.experimental.pallas import tpu as pltpu

F32 = jnp.float32
BF16 = jnp.bfloat16

LANES = 128
SUBLANES = 8
MXU_TILE = 256
VMEM_LIMIT_BYTES = 56 * 1024 * 1024

D_MODEL = 1024
HEAD_DIM = 64
A_HEADS = 4
B_HEADS = 4
C_HEADS = 8
C_KV_HEADS = 2
A_WIDTH = A_HEADS * HEAD_DIM
B_WIDTH = B_HEADS * HEAD_DIM
C_WIDTH = C_HEADS * HEAD_DIM
C_KV_WIDTH = C_KV_HEADS * HEAD_DIM
DILATED_PAIRS = ((128, 1), (512, 4), (2048, 16))
BAND = 64
ROPE_THETA = 500000.0
ROPE_DIMS = HEAD_DIM // 4
AXIAL_THETA = 10000.0
GRID_W = 64
N_MEM = 256
X_HEADS = 4
X_WIDTH = 256
D_FF = 2752
D_FF_PAD = 2816
RMS_EPS = 1e-6
NEG_INF = -1e30
Q_SCALE = HEAD_DIM ** -0.5
LOG2_E = 1.4426950408889634

COL_A = 0
COL_BQK = COL_A + 3 * A_WIDTH
COL_BV = COL_BQK + 2 * B_WIDTH
COL_BO = COL_BV + B_WIDTH
COL_CQ = COL_BO + B_WIDTH
COL_CK = COL_CQ + C_WIDTH
COL_CV = COL_CK + C_KV_WIDTH
COL_G = COL_CV + C_KV_WIDTH
IN_COLS = COL_G + LANES
N_GATE = 2 * B_HEADS

TQ_DIL = 128
DIL_UNROLL = 8
MLSTM_CHUNK = 256
TQ_GQA = 256
KC_GQA = 1024
FF_CHUNK = 256


def _dot(a, b):
    return jnp.dot(a, b, preferred_element_type=F32)


def _dot_nt(a, b):
    return lax.dot_general(a, b, (((1,), (1,)), ((), ())), preferred_element_type=F32)


def _split2(x):
    hi = x.astype(BF16)
    return hi, (x - hi.astype(F32)).astype(BF16)


def _split3(x):
    hi = x.astype(BF16)
    r = x - hi.astype(F32)
    mid = r.astype(BF16)
    return hi, mid, (r - mid.astype(F32)).astype(BF16)


def _head_mean(xx, bd):
    w = xx.shape[1]
    tile = bd.shape[0]
    xb = xx.astype(BF16)
    if w <= tile:
        return _dot(xb, bd[:w, :w])
    return jnp.concatenate([_dot(xb[:, t:t + tile], bd) for t in range(0, w, tile)], axis=1)


def _rms(x, g):
    return x * lax.rsqrt(jnp.mean(x * x, axis=-1, keepdims=True) + RMS_EPS) * g


def _lane_iota(shape):
    return lax.broadcasted_iota(jnp.int32, shape, len(shape) - 1)


def _head_mask(width, h, dtype):
    lane = _lane_iota((1, width))
    return ((lane >= h * HEAD_DIM) & (lane < (h + 1) * HEAD_DIM)).astype(dtype)


def _rope(x, tab_ref, shift):
    w = x.shape[1]
    rep = w // LANES
    tile = lambda t: jnp.concatenate([t] * rep, axis=1) if rep > 1 else t
    c, sm, sp = tile(tab_ref[0]), tile(tab_ref[1]), tile(tab_ref[2])
    return x * c + pltpu.roll(x, w - shift, axis=1) * sm + pltpu.roll(x, shift, axis=1) * sp


def _shift_rows(z, prev_row, next_row):
    n = z.shape[0]
    row = lax.broadcasted_iota(jnp.int32, z.shape, 0)
    zp = jnp.where(row == 0, prev_row, pltpu.roll(z, 1, axis=0))
    zn = jnp.where(row == n - 1, next_row, pltpu.roll(z, n - 1, axis=0))
    return zp, zn


def _params(sem):
    return pltpu.CompilerParams(dimension_semantics=sem, vmem_limit_bytes=VMEM_LIMIT_BYTES)


def _const_spec(shape):
    n = len(shape)
    return pl.BlockSpec(shape, lambda *_: (0,) * n)


def _mem_kv_kernel(mem_ref, g_ref, w_ref, kt_ref, v_ref):
    hn = _rms(mem_ref[0], g_ref[...]).astype(BF16)
    kv = _dot(hn, w_ref[...])
    kt_ref[0] = kv[:, :X_WIDTH].T.astype(BF16)
    v_ref[0] = kv[:, X_WIDTH:].astype(BF16)


def _mem_kv(mem, g, w):
    b = mem.shape[0]
    return pl.pallas_call(
        _mem_kv_kernel,
        grid=(b,),
        in_specs=[pl.BlockSpec((1, N_MEM, D_MODEL), lambda i: (i, 0, 0)),
                  _const_spec((1, D_MODEL)), _const_spec((D_MODEL, 2 * X_WIDTH))],
        out_specs=[pl.BlockSpec((1, X_WIDTH, N_MEM), lambda i: (i, 0, 0)),
                   pl.BlockSpec((1, N_MEM, X_WIDTH), lambda i: (i, 0, 0))],
        out_shape=[jax.ShapeDtypeStruct((b, X_WIDTH, N_MEM), BF16),
                   jax.ShapeDtypeStruct((b, N_MEM, X_WIDTH), BF16)],
        compiler_params=_params(("arbitrary",)),
        name="mem_kv",
    )(mem, g, w)


def _in_proj_kernel(x_ref, xp_ref, xn_ref, g_ref, w_ref, cw_ref, cb_ref, ks_ref, ta_ref, tc_ref, qg_ref, kg_ref,
                    bd_ref, a1_ref, a4_ref, a16_ref, bqk_ref, bv_ref, bo_ref, bg_ref, cq_ref, ckv_ref, asc_ref):
    i = pl.program_id(1)
    last = pl.num_programs(1) - 1
    g = g_ref[...]
    hn = _rms(x_ref[0], g).astype(BF16)
    tm = hn.shape[0]

    zc = _dot(hn, w_ref[:, COL_CQ:COL_G])
    cq = zc[:, :C_WIDTH]
    cq = cq * lax.rsqrt(_head_mean(cq * cq, bd_ref[...]) + RMS_EPS) * qg_ref[...]
    cq_ref[0] = _rope(cq, tc_ref, HEAD_DIM // 4).astype(BF16)
    ck = zc[:, C_WIDTH:C_WIDTH + C_KV_WIDTH]
    ck = ck * lax.rsqrt(_head_mean(ck * ck, bd_ref[...]) + RMS_EPS) * kg_ref[...]
    ckv_ref[0, :, :C_KV_WIDTH] = _rope(ck, tc_ref, HEAD_DIM // 4).astype(BF16)
    ckv_ref[0, :, C_KV_WIDTH:] = zc[:, C_WIDTH + C_KV_WIDTH:].astype(BF16)

    za = _dot(hn, w_ref[:, COL_A:COL_BQK])
    av = jnp.concatenate([_rope(za[:, :2 * A_WIDTH], ta_ref, ROPE_DIMS // 2), za[:, 2 * A_WIDTH:]], axis=1)
    a1_ref[0, 0] = av.astype(BF16)
    for c in range(3 * A_WIDTH // LANES):
        asc_ref[c] = av[:, c * LANES:(c + 1) * LANES]
    for dil, ref in ((DILATED_PAIRS[1][1], a4_ref), (DILATED_PAIRS[2][1], a16_ref)):
        for r in range(dil):
            for c in range(3 * A_WIDTH // LANES):
                ref[0, r, :, c * LANES:(c + 1) * LANES] = asc_ref[c, pl.ds(r, tm // dil, stride=dil), :].astype(BF16)

    wb = w_ref[:, COL_BQK:COL_BV]
    zb = _dot(hn, wb)
    zprev = _dot(_rms(xp_ref[0], g).astype(BF16), wb)[SUBLANES - 1:SUBLANES] * (i > 0).astype(F32)
    znext = _dot(_rms(xn_ref[0], g).astype(BF16), wb)[0:1] * (i < last).astype(F32)
    zp, zn = _shift_rows(zb, zprev, znext)
    conv = zp * cw_ref[0:1] + zb * cw_ref[1:2] + zn * cw_ref[2:3] + cb_ref[...]
    bqk_ref[0] = (conv * jax.nn.sigmoid(conv) * ks_ref[...]).astype(BF16)
    zrest = _dot(hn, w_ref[:, COL_BV:COL_CQ])
    bv_ref[0] = zrest[:, :B_WIDTH].astype(BF16)
    bo_ref[0] = zrest[:, B_WIDTH:]
    bg_ref[0] = _dot(hn, w_ref[:, COL_G:IN_COLS])


def _in_proj(x, g, w, cw, cb, kscale, ta, tc, qg, kg, bd, tm):
    b, s, _ = x.shape
    nt = s // tm
    hb = tm // SUBLANES
    tok = lambda width: pl.BlockSpec((1, tm, width), lambda bi, i: (bi, i, 0))
    out_widths = (2 * B_WIDTH, B_WIDTH, B_WIDTH, LANES, C_WIDTH, 2 * C_KV_WIDTH)
    out_dtypes = (BF16, BF16, F32, F32, BF16, BF16)
    dils = [d for _, d in DILATED_PAIRS]
    a_specs = [pl.BlockSpec((1, d, tm // d, 3 * A_WIDTH), lambda bi, i: (bi, 0, i, 0)) for d in dils]
    a_shapes = [jax.ShapeDtypeStruct((b, d, s // d, 3 * A_WIDTH), BF16) for d in dils]
    return pl.pallas_call(
        _in_proj_kernel,
        grid=(b, nt),
        in_specs=[tok(D_MODEL),
                  pl.BlockSpec((1, SUBLANES, D_MODEL), lambda bi, i: (bi, jnp.maximum(i * hb - 1, 0), 0)),
                  pl.BlockSpec((1, SUBLANES, D_MODEL), lambda bi, i: (bi, jnp.minimum((i + 1) * hb, s // SUBLANES - 1), 0)),
                  _const_spec((1, D_MODEL)), _const_spec((D_MODEL, IN_COLS)),
                  _const_spec((3, 2 * B_WIDTH)), _const_spec((1, 2 * B_WIDTH)), _const_spec((1, 2 * B_WIDTH)),
                  pl.BlockSpec((3, tm, LANES), lambda bi, i: (0, i, 0)),
                  pl.BlockSpec((3, tm, LANES), lambda bi, i: (0, i, 0)),
                  _const_spec((1, C_WIDTH)), _const_spec((1, C_KV_WIDTH)), _const_spec((MXU_TILE, MXU_TILE))],
        out_specs=a_specs + [tok(wd) for wd in out_widths],
        out_shape=a_shapes + [jax.ShapeDtypeStruct((b, s, wd), dt) for wd, dt in zip(out_widths, out_dtypes)],
        scratch_shapes=[pltpu.VMEM((3 * A_WIDTH // LANES, tm, LANES), F32)],
        compiler_params=_params(("arbitrary", "arbitrary")),
        name="in_proj",
    )(x, x, x, g, w, cw, cb, kscale, ta, tc, qg, kg, bd)


def _dilated_kernel(a_ref, o_ref, l_ref, *, length, n_res, win):
    n_blk = length // TQ_DIL
    masks = [_head_mask(A_WIDTH, h, BF16) for h in range(A_HEADS)]
    lane = _lane_iota((1, A_WIDTH))
    rel = (lax.broadcasted_iota(jnp.int32, (1, TQ_DIL, win), 2)
           - lax.broadcasted_iota(jnp.int32, (1, TQ_DIL, win), 1))

    def pick(parts):
        out = parts[A_HEADS - 1]
        for h in range(A_HEADS - 2, -1, -1):
            out = jnp.where(lane < (h + 1) * HEAD_DIM, parts[h], out)
        return out

    for r in range(n_res):

        def block(jb, carry, r=r):
            if isinstance(jb, int):
                j0 = jb * TQ_DIL
                ws = min(max(j0 - BAND, 0), length - win)
            else:
                j0 = pl.multiple_of(jb * TQ_DIL, TQ_DIL)
                ws = pl.multiple_of(jnp.clip(j0 - BAND, 0, length - win), BAND)
            q = a_ref[0, r, pl.ds(j0, TQ_DIL), :A_WIDTH]
            kw = a_ref[0, r, pl.ds(ws, win), A_WIDTH:2 * A_WIDTH]
            vw = a_ref[0, r, pl.ds(ws, win), 2 * A_WIDTH:]
            q4 = jnp.concatenate([q * m for m in masks], axis=0)
            s = _dot_nt(q4, kw).reshape(A_HEADS, TQ_DIL, win)
            s = jnp.where(jnp.abs(rel + (ws - j0)) <= BAND, s, NEG_INF)
            m = jnp.max(s, axis=-1, keepdims=True)
            p = jnp.exp(s - m)
            l = jnp.sum(p, axis=-1, keepdims=True)
            pn = (p / l).astype(BF16).reshape(A_HEADS * TQ_DIL, win)
            o4 = _dot(pn, vw).reshape(A_HEADS, TQ_DIL, A_WIDTH)
            lse = m + jnp.log(l)
            o_ref[0, r, pl.ds(j0, TQ_DIL), :] = pick([o4[h] for h in range(A_HEADS)])
            l_ref[0, r, pl.ds(j0, TQ_DIL), :] = pick(
                [jnp.broadcast_to(lse[h], (TQ_DIL, A_WIDTH)) for h in range(A_HEADS)])
            return carry

        if n_blk == 1:
            block(0, 0)
        else:
            lax.fori_loop(0, n_blk, block, 0, unroll=min(n_blk, DIL_UNROLL))


def _dilated_branch(a):
    b, dilation, length, _ = a.shape
    n_res = max(1, min(dilation, 2048 // length))
    win = min(length, TQ_DIL + 2 * BAND)
    out_spec = pl.BlockSpec((1, n_res, length, A_WIDTH), lambda bi, ri: (bi, ri, 0, 0))
    return pl.pallas_call(
        functools.partial(_dilated_kernel, length=length, n_res=n_res, win=win),
        grid=(b, dilation // n_res),
        in_specs=[pl.BlockSpec((1, n_res, length, 3 * A_WIDTH), lambda bi, ri: (bi, ri, 0, 0))],
        out_specs=[out_spec, out_spec],
        out_shape=[jax.ShapeDtypeStruct((b, dilation, length, A_WIDTH), F32)] * 2,
        compiler_params=_params(("arbitrary", "arbitrary")),
        name=f"dilated_d{dilation}",
    )(a)


def _cummax_rows(x, reverse):
    n = x.shape[0]
    row = lax.broadcasted_iota(jnp.int32, x.shape, 0)
    step = 1
    while step < n:
        if reverse:
            x = jnp.maximum(x, jnp.where(row < n - step, pltpu.roll(x, n - step, axis=0), NEG_INF))
        else:
            x = jnp.maximum(x, jnp.where(row >= step, pltpu.roll(x, step, axis=0), NEG_INF))
        step *= 2
    return x


def _mlstm_kernel(qk_ref, v_ref, o_ref, g_ref, gb_ref, ng_ref, tl_ref, tu_ref, bd_ref, out_ref,
                  hbuf_ref, c_ref, n_ref, m_ref):
    lc = MLSTM_CHUNK
    n_chunks = qk_ref.shape[1] // lc
    lane = _lane_iota((1, B_WIDTH))
    glane = _lane_iota((1, LANES))
    head_masks = [_head_mask(B_WIDTH, h, BF16) for h in range(B_HEADS)]
    row_i = lax.broadcasted_iota(jnp.int32, (lc, lc), 0)
    col_j = lax.broadcasted_iota(jnp.int32, (lc, lc), 1)
    feat = lax.broadcasted_iota(jnp.int32, (B_WIDTH, B_WIDTH), 0) // HEAD_DIM
    bd_mask = feat == lax.broadcasted_iota(jnp.int32, (B_WIDTH, B_WIDTH), 1) // HEAD_DIM
    nfeat = lax.broadcasted_iota(jnp.int32, (B_WIDTH, LANES), 0) // HEAD_DIM
    nlane = lax.broadcasted_iota(jnp.int32, (B_WIDTH, LANES), 1)

    head_of_lane = lax.broadcasted_iota(jnp.int32, (lc, LANES), 1) // HEAD_DIM

    def spread(cols):
        out = cols[B_HEADS - 1]
        for h in range(B_HEADS - 2, -1, -1):
            out = jnp.where(lane < (h + 1) * HEAD_DIM, cols[h], out)
        return out

    def chunk(c, reverse):
        base = B_HEADS if reverse else 0
        r0 = pl.multiple_of(c * lc, lc)
        q = qk_ref[0, pl.ds(r0, lc), :B_WIDTH]
        k = qk_ref[0, pl.ds(r0, lc), B_WIDTH:]
        v = v_ref[0, pl.ds(r0, lc), :]
        gates = g_ref[0, pl.ds(r0, lc), :] + gb_ref[...]
        ig = gates
        lsig = jnp.minimum(gates, 0.0) - jnp.log1p(jnp.exp(-jnp.abs(gates)))
        lf = pltpu.roll(lsig, LANES - N_GATE, axis=1)
        tri = (tu_ref if reverse else tl_ref)[...]
        hi, mid, lo = _split3(lf)
        bcol = _dot(tri, hi) + _dot(tri, mid) + _dot(tri, lo)
        tot = jnp.sum(lf, axis=0, keepdims=True)
        rcol = ig - bcol
        gend = tot + rcol
        d = int(reverse)
        m_prev = m_ref[d]
        m_new = jnp.maximum(tot + m_prev, jnp.max(gend, axis=0, keepdims=True))
        wend = jnp.exp(gend - m_new)
        decay = jnp.exp(tot + m_prev - m_new)
        acol = -jnp.maximum(m_prev, _cummax_rows(rcol, reverse))
        wstate = jnp.exp(m_prev + acol)
        floor_ = jnp.exp(acol - bcol)
        a2 = acol * LOG2_E
        r2_t = (rcol * LOG2_E).T
        k_t = k.astype(F32).T.astype(BF16)
        q_c = _dot(q, c_ref[d].astype(BF16))
        q_n = _dot(q, n_ref[d].astype(BF16))
        mask = (col_j >= row_i) if reverse else (col_j <= row_i)
        ws, sum_w = [], jnp.zeros((lc, LANES), F32)
        for h in range(B_HEADS):
            dh = base + h
            sc = _dot(q * head_masks[h], k_t)
            e = jnp.exp2(jnp.where(mask, a2[:, dh:dh + 1] + r2_t[dh:dh + 1, :], NEG_INF))
            w = e * sc
            sum_w = jnp.where(glane == dh, jnp.sum(w, axis=1, keepdims=True), sum_w)
            ws.append(w.astype(BF16))
        den = wstate * q_n + sum_w
        rinv = 1.0 / jnp.maximum(jnp.abs(den), floor_)
        def heads(t):
            if t.shape[0] != lc:
                return spread([t[:, base + h:base + h + 1] for h in range(B_HEADS)])
            return jnp.concatenate(
                [jnp.take_along_axis(t, base + tile * (LANES // HEAD_DIM) + head_of_lane, axis=1)
                 for tile in range(B_WIDTH // LANES)], axis=1)
        pv = _dot(jnp.concatenate(ws, axis=1), jnp.concatenate([v * hm for hm in head_masks], axis=0))
        h_out = (heads(wstate) * q_c + pv) * heads(rinv)
        upd = _dot(k_t, (v.astype(F32) * heads(wend)).astype(BF16))
        c_ref[d] = heads(decay) * c_ref[d] + jnp.where(bd_mask, upd, 0.0)
        nupd = _dot(k_t, wend.astype(BF16))
        n_ref[d] = decay * n_ref[d] + jnp.where(nfeat == nlane - base, nupd, 0.0)
        m_ref[d] = m_new
        return r0, h_out

    def finish(r0, hsum):
        hn = hsum * lax.rsqrt(_head_mean(hsum * hsum, bd_ref[...]) + RMS_EPS) * ng_ref[...]
        out_ref[0, pl.ds(r0, lc), :] = (jax.nn.sigmoid(o_ref[0, pl.ds(r0, lc), :]) * hn).astype(BF16)

    c_ref[...] = jnp.zeros_like(c_ref)
    n_ref[...] = jnp.zeros_like(n_ref)
    m_ref[...] = jnp.zeros_like(m_ref)

    def first_half(t, carry):
        for reverse in (False, True):
            r0, h_out = chunk(n_chunks - 1 - t if reverse else t, reverse)
            hbuf_ref[pl.ds(r0, lc), :] = h_out
        return carry

    def second_half(t, carry):
        for reverse in (False, True):
            r0, h_out = chunk(n_chunks - 1 - t if reverse else t, reverse)
            finish(r0, h_out + hbuf_ref[pl.ds(r0, lc), :])
        return carry

    lax.fori_loop(0, n_chunks // 2, first_half, 0)
    lax.fori_loop(n_chunks // 2, n_chunks, second_half, 0)


def _mlstm(bqk, bv, bo, bg, gbias, ng, tl, tu, bd):
    b, s, _ = bqk.shape
    assert s % (2 * MLSTM_CHUNK) == 0
    row = lambda width: pl.BlockSpec((1, s, width), lambda i: (i, 0, 0))
    return pl.pallas_call(
        _mlstm_kernel,
        grid=(b,),
        in_specs=[row(2 * B_WIDTH), row(B_WIDTH), row(B_WIDTH), row(LANES),
                  _const_spec((1, LANES)), _const_spec((1, B_WIDTH)),
                  _const_spec((MLSTM_CHUNK, MLSTM_CHUNK)), _const_spec((MLSTM_CHUNK, MLSTM_CHUNK)),
                  _const_spec((MXU_TILE, MXU_TILE))],
        out_specs=row(B_WIDTH),
        out_shape=jax.ShapeDtypeStruct((b, s, B_WIDTH), BF16),
        scratch_shapes=[pltpu.VMEM((s, B_WIDTH), F32), pltpu.VMEM((2, B_WIDTH, B_WIDTH), F32),
                        pltpu.VMEM((2, B_WIDTH, LANES), F32), pltpu.VMEM((2, 1, LANES), F32)],
        compiler_params=_params(("arbitrary",)),
        name="mlstm",
    )(bqk, bv, bo, bg, gbias, ng, tl, tu, bd)


def _gqa_kernel(q_ref, kv_ref, o_ref, kt_ref, vx_ref):
    s = kv_ref.shape[1]
    n_kc = s // KC_GQA
    lo = _lane_iota((1, LANES)) < HEAD_DIM
    lo_b = lo.astype(BF16)
    hi_b = 1.0 - lo_b

    @pl.when(pl.program_id(1) == 0)
    def _():
        def prep(c, carry):
            r0 = pl.multiple_of(c * KC_GQA, KC_GQA)
            kc = kv_ref[0, pl.ds(r0, KC_GQA), :C_KV_WIDTH].astype(F32)
            kr = pltpu.roll(kc, HEAD_DIM, axis=1)
            kt_ref[0, c] = jnp.where(lo, kc, kr).T.astype(BF16)
            kt_ref[1, c] = jnp.where(lo, kr, kc).T.astype(BF16)
            vc = kv_ref[0, pl.ds(r0, KC_GQA), C_KV_WIDTH:].astype(F32)
            vr = pltpu.roll(vc, HEAD_DIM, axis=1)
            vx_ref[0, pl.ds(r0, KC_GQA), :] = jnp.where(lo, vc, 0.0).astype(BF16)
            vx_ref[1, pl.ds(r0, KC_GQA), :] = jnp.where(lo, 0.0, vr).astype(BF16)
            vx_ref[2, pl.ds(r0, KC_GQA), :] = jnp.where(lo, vr, 0.0).astype(BF16)
            vx_ref[3, pl.ds(r0, KC_GQA), :] = jnp.where(lo, 0.0, vc).astype(BF16)
            return carry

        lax.fori_loop(0, n_kc, prep, 0)

    tq = q_ref.shape[1]
    per_grp = C_HEADS // C_KV_HEADS
    for grp in range(C_KV_HEADS):
        pairs = [grp * per_grp // 2 + j for j in range(per_grp // 2)]
        qs = []
        for pair in pairs:
            qp = q_ref[0, :, pair * LANES:(pair + 1) * LANES]
            qs += [qp * lo_b, qp * hi_b]

        def body(c, carry, qs=qs, grp=grp):
            ms, ls, accs = carry
            kt = kt_ref[grp, c]
            r0 = pl.multiple_of(c * KC_GQA, KC_GQA)
            new_m, new_l, alphas, ps = [], [], [], []
            for h in range(per_grp):
                sc = _dot(qs[h], kt)
                m_new = jnp.maximum(ms[h], jnp.max(sc, axis=1, keepdims=True))
                alpha = jnp.exp2(ms[h] - m_new)
                p = jnp.exp2(sc - m_new)
                new_m.append(m_new)
                new_l.append(alpha * ls[h] + jnp.sum(p, axis=1, keepdims=True))
                alphas.append(alpha)
                ps.append(p.astype(BF16))
            new_acc = []
            for j in range(per_grp // 2):
                scale = jnp.where(lo, alphas[2 * j], alphas[2 * j + 1])
                new_acc.append(scale * accs[j]
                               + _dot(ps[2 * j], vx_ref[2 * grp, pl.ds(r0, KC_GQA), :])
                               + _dot(ps[2 * j + 1], vx_ref[2 * grp + 1, pl.ds(r0, KC_GQA), :]))
            return tuple(new_m), tuple(new_l), tuple(new_acc)

        init = (tuple(jnp.full((tq, 1), NEG_INF, F32) for _ in range(per_grp)),
                tuple(jnp.zeros((tq, 1), F32) for _ in range(per_grp)),
                tuple(jnp.zeros((tq, LANES), F32) for _ in range(per_grp // 2)))
        _, ls, accs = lax.fori_loop(0, n_kc, body, init, unroll=True)
        for j, pair in enumerate(pairs):
            o_ref[0, :, pair * LANES:(pair + 1) * LANES] = (
                accs[j] / jnp.where(lo, ls[2 * j], ls[2 * j + 1])).astype(BF16)


def _gqa(cq, ckv):
    b, s, _ = cq.shape
    return pl.pallas_call(
        _gqa_kernel,
        grid=(b, s // TQ_GQA),
        in_specs=[pl.BlockSpec((1, TQ_GQA, C_WIDTH), lambda bi, i: (bi, i, 0)),
                  pl.BlockSpec((1, s, 2 * C_KV_WIDTH), lambda bi, i: (bi, 0, 0))],
        out_specs=pl.BlockSpec((1, TQ_GQA, C_WIDTH), lambda bi, i: (bi, i, 0)),
        out_shape=jax.ShapeDtypeStruct((b, s, C_WIDTH), BF16),
        scratch_shapes=[pltpu.VMEM((C_KV_HEADS, s // KC_GQA, LANES, KC_GQA), BF16),
                        pltpu.VMEM((2 * C_KV_HEADS, s, LANES), BF16)],
        compiler_params=_params(("arbitrary", "arbitrary")),
        name="gqa",
    )(cq, ckv)


def _out_cross_kernel(x_ref, o1_ref, l1_ref, o2_ref, l2_ref, o3_ref, l3_ref, yb_ref, yc_ref, wo_ref,
                      gx_ref, wq_ref, kt_ref, vm_ref, wxo_ref, out_ref, il_ref):
    tm = x_ref.shape[1]

    def tokens(ref, slot):
        d = ref.shape[1]
        if d == 1:
            return ref[0, 0]
        tiles = A_WIDTH // LANES
        for r in range(d):
            for c in range(tiles):
                il_ref[slot * tiles + c, pl.ds(r, tm // d, stride=d), :] = ref[0, r, :, c * LANES:(c + 1) * LANES]
        return jnp.concatenate([il_ref[slot * tiles + c] for c in range(tiles)], axis=1)

    l1, l2, l3 = tokens(l1_ref, 0), tokens(l2_ref, 0), tokens(l3_ref, 1)
    o1, o2, o3 = tokens(o1_ref, 0), tokens(o2_ref, 2), tokens(o3_ref, 3)
    mx = jnp.maximum(jnp.maximum(l1, l2), l3)
    e1, e2, e3 = jnp.exp(l1 - mx), jnp.exp(l2 - mx), jnp.exp(l3 - mx)
    ya = (e1 * o1 + e2 * o2 + e3 * o3) / (e1 + e2 + e3)
    cat = jnp.concatenate([ya.astype(BF16), yb_ref[0], yc_ref[0]], axis=1)
    x1 = x_ref[0] + _dot(cat, wo_ref[...])

    tm = x1.shape[0]
    q = _dot(_rms(x1, gx_ref[...]).astype(BF16), wq_ref[...]).astype(BF16)
    q4 = jnp.concatenate([q * _head_mask(X_WIDTH, h, BF16) for h in range(X_HEADS)], axis=0)
    s = _dot(q4, kt_ref[0])
    m = jnp.max(s, axis=1, keepdims=True)
    p = jnp.exp(s - m)
    pn = (p / jnp.sum(p, axis=1, keepdims=True)).astype(BF16)
    o4 = _dot(pn, vm_ref[0])
    lane = _lane_iota((1, X_WIDTH))
    o = o4[(X_HEADS - 1) * tm:]
    for h in range(X_HEADS - 2, -1, -1):
        o = jnp.where(lane < (h + 1) * HEAD_DIM, o4[h * tm:(h + 1) * tm], o)
    out_ref[0] = x1 + _dot(o.astype(BF16), wxo_ref[...])


def _out_cross(x, branches, yb, yc, wo, gx, wq, kt, vm, wxo, tm):
    b, s, _ = x.shape
    tok = lambda width: pl.BlockSpec((1, tm, width), lambda bi, i: (bi, i, 0))
    per_b = lambda shape: pl.BlockSpec((1,) + shape, lambda bi, i: (bi, 0, 0))
    flat = [t for pair in branches for t in pair]
    res = lambda t: pl.BlockSpec((1, t.shape[1], tm // t.shape[1], A_WIDTH), lambda bi, i: (bi, 0, i, 0))
    return pl.pallas_call(
        _out_cross_kernel,
        grid=(b, s // tm),
        in_specs=[tok(D_MODEL)] + [res(t) for t in flat] + [tok(B_WIDTH), tok(C_WIDTH),
                  _const_spec((D_MODEL, D_MODEL)), _const_spec((1, D_MODEL)), _const_spec((D_MODEL, X_WIDTH)),
                  per_b((X_WIDTH, N_MEM)), per_b((N_MEM, X_WIDTH)), _const_spec((X_WIDTH, D_MODEL))],
        out_specs=tok(D_MODEL),
        out_shape=jax.ShapeDtypeStruct((b, s, D_MODEL), F32),
        scratch_shapes=[pltpu.VMEM((4 * A_WIDTH // LANES, tm, LANES), F32)],
        compiler_params=_params(("arbitrary", "arbitrary")),
        name="out_cross",
    )(x, *flat, yb, yc, wo, gx, wq, kt, vm, wxo)


def _conv_ffn_kernel(x_ref, xp_ref, xn_ref, g_ref, wg_ref, wv_ref, cw_ref, cb_ref, wd_ref, fg_ref, out_ref, act_ref,
                     *, final_norm):
    i = pl.program_id(1)
    last = pl.num_programs(1) - 1
    g = g_ref[...]
    x = x_ref[0]
    tm = x.shape[0]
    hn = _rms(x, g).astype(BF16)
    hp = (_rms(xp_ref[0], g) * (i > 0).astype(F32)).astype(BF16)
    hx = (_rms(xn_ref[0], g) * (i < last).astype(F32)).astype(BF16)
    h_ext = jnp.concatenate([hp, hn, hx], axis=0)
    for c in range(D_FF_PAD // FF_CHUNK):
        cols = slice(c * FF_CHUNK, (c + 1) * FF_CHUNK)
        zg = _dot(h_ext, wg_ref[:, cols])
        ext = tm + 2 * SUBLANES
        zp = pltpu.roll(zg, 1, axis=0)[SUBLANES:SUBLANES + tm]
        zn = pltpu.roll(zg, ext - 1, axis=0)[SUBLANES:SUBLANES + tm]
        conv = (zp * cw_ref[0:1, cols] + zg[SUBLANES:SUBLANES + tm] * cw_ref[1:2, cols]
                + zn * cw_ref[2:3, cols] + cb_ref[:, cols])
        act_ref[:, cols] = (conv * jax.nn.sigmoid(conv) * _dot(hn, wv_ref[:, cols])).astype(BF16)
    y = x + _dot(act_ref[...], wd_ref[...])
    if final_norm:
        y = _rms(y, fg_ref[...])
    out_ref[0] = y


def _conv_ffn(x, g, wg, wv, cw, cb, wd, fg, tm, final_norm):
    b, s, _ = x.shape
    hb = tm // SUBLANES
    tok = pl.BlockSpec((1, tm, D_MODEL), lambda bi, i: (bi, i, 0))
    once = lambda shape: pl.BlockSpec(shape, lambda *_: (0,) * len(shape), pipeline_mode=pl.Buffered(1))
    return pl.pallas_call(
        functools.partial(_conv_ffn_kernel, final_norm=final_norm),
        grid=(b, s // tm),
        in_specs=[tok,
                  pl.BlockSpec((1, SUBLANES, D_MODEL), lambda bi, i: (bi, jnp.maximum(i * hb - 1, 0), 0)),
                  pl.BlockSpec((1, SUBLANES, D_MODEL), lambda bi, i: (bi, jnp.minimum((i + 1) * hb, s // SUBLANES - 1), 0)),
                  _const_spec((1, D_MODEL)), once((D_MODEL, D_FF_PAD)), once((D_MODEL, D_FF_PAD)),
                  _const_spec((3, D_FF_PAD)), _const_spec((1, D_FF_PAD)), once((D_FF_PAD, D_MODEL)),
                  _const_spec((1, D_MODEL))],
        out_specs=tok,
        out_shape=jax.ShapeDtypeStruct((b, s, D_MODEL), F32),
        scratch_shapes=[pltpu.VMEM((tm, D_FF_PAD), BF16)],
        compiler_params=_params(("arbitrary", "arbitrary")),
        name="conv_ffn",
    )(x, x, x, g, wg, wv, cw, cb, wd, fg)


def _rope_tables(positions, dim, theta):
    half = dim // 2
    s = positions[0].shape[0]
    inv = theta ** (-jnp.arange(0, dim, 2, dtype=F32) / dim)
    zeros = jnp.zeros((s, half), F32)
    c, sm, sp = [], [], []
    for pos in positions:
        ang = pos.astype(F32)[:, None] * inv[None, :]
        cos, sin = jnp.cos(ang), jnp.sin(ang)
        c += [cos, cos]
        sm += [-sin, zeros]
        sp += [zeros, sin]
    rest = HEAD_DIM - dim * len(positions)
    if rest:
        c.append(jnp.ones((s, rest), F32))
        sm.append(jnp.zeros((s, rest), F32))
        sp.append(jnp.zeros((s, rest), F32))
    two = lambda parts: jnp.concatenate(parts + parts, axis=1)
    return jnp.stack([two(c), two(sm), two(sp)], axis=0)


def _block_diag_mean(width):
    blk = np.arange(width) // HEAD_DIM
    return jnp.asarray((blk[:, None] == blk[None, :]).astype(np.float32) / HEAD_DIM, BF16)


def _prep_layer(p, l):
    splits = np.cumsum((A_WIDTH, A_WIDTH, A_WIDTH, 2 * B_WIDTH, B_WIDTH, B_WIDTH, 4 * B_HEADS,
                        C_WIDTH, C_KV_WIDTH, C_KV_WIDTH))[:-1].tolist()
    a_q, a_k, a_v, b_qk, b_v, b_o, b_g, c_q, c_k, c_v = jnp.split(p["w_in"][l], splits, axis=1)
    g4 = b_g.reshape(D_MODEL, 2, 2, B_HEADS)
    gates = jnp.concatenate([g4[:, :, 0].reshape(D_MODEL, N_GATE), g4[:, :, 1].reshape(D_MODEL, N_GATE),
                             jnp.zeros((D_MODEL, LANES - 2 * N_GATE), F32)], axis=1)
    w_in = jnp.concatenate([a_q * Q_SCALE, a_k, a_v, b_qk, b_v, b_o, c_q, c_k, c_v, gates], axis=1).astype(BF16)
    gbias = jnp.concatenate([p["mlstm_igate_b"][l].reshape(N_GATE), p["mlstm_fgate_b"][l].reshape(N_GATE),
                             jnp.zeros((LANES - 2 * N_GATE,), F32)]).reshape(1, LANES)
    kscale = jnp.concatenate([jnp.ones((B_WIDTH,), F32), jnp.full((B_WIDTH,), Q_SCALE, F32)]).reshape(1, 2 * B_WIDTH)
    pad_c = lambda w: jnp.pad(w, ((0, 0), (0, D_FF_PAD - D_FF)))
    w_up = p["w_ffn_up"][l]
    return dict(
        g_mix=p["norm_mix_g"][l].reshape(1, D_MODEL), w_in=w_in,
        cw=p["mlstm_conv_w"][l], cb=p["mlstm_conv_b"][l].reshape(1, 2 * B_WIDTH), kscale=kscale, gbias=gbias,
        ng=p["mlstm_norm_g"][l].reshape(1, B_WIDTH),
        qg=jnp.tile(p["qk_norm_g"][l, 0] * (Q_SCALE * LOG2_E), C_HEADS).reshape(1, C_WIDTH),
        kg=jnp.tile(p["qk_norm_g"][l, 1], C_KV_HEADS).reshape(1, C_KV_WIDTH),
        w_out=p["w_out"][l].astype(BF16),
        g_x=p["norm_x_g"][l].reshape(1, D_MODEL), g_mem=p["norm_mem_g"][l].reshape(1, D_MODEL),
        w_xq=(p["w_xq"][l] * Q_SCALE).astype(BF16), w_xkv=p["w_xkv"][l].astype(BF16), w_xo=p["w_xo"][l].astype(BF16),
        g_ffn=p["norm_ffn_g"][l].reshape(1, D_MODEL),
        w_gate=pad_c(w_up[:, :D_FF]).astype(BF16), w_val=pad_c(w_up[:, D_FF:]).astype(BF16),
        fcw=pad_c(p["ffn_conv_w"][l]), fcb=pad_c(p["ffn_conv_b"][l].reshape(1, D_FF)),
        w_down=jnp.pad(p["w_ffn_down"][l], ((0, D_FF_PAD - D_FF), (0, 0))).astype(BF16),
    )


def _trunk(x, mem, layers, final_g, consts):
    b, s, _ = x.shape
    pos = jnp.arange(s)
    ta = _rope_tables([pos], ROPE_DIMS, ROPE_THETA)
    tc = _rope_tables([pos // GRID_W, pos % GRID_W], HEAD_DIM // 2, AXIAL_THETA)
    depth = len(layers)
    for l, w in enumerate(layers):
        kt, vm = _mem_kv(mem, w["g_mem"], w["w_xkv"])
        a1, a4, a16, bqk, bv, bo, bg, cq, ckv = _in_proj(x, w["g_mix"], w["w_in"], w["cw"], w["cb"], w["kscale"],
                                                         ta, tc, w["qg"], w["kg"], consts["bd"], tm=512)
        branches = [_dilated_branch(a) for a in (a1, a4, a16)]
        yb = _mlstm(bqk, bv, bo, bg, w["gbias"], w["ng"], consts["tl"], consts["tu"], consts["bd"])
        yc = _gqa(cq, ckv)
        x = _out_cross(x, branches, yb, yc, w["w_out"], w["g_x"], w["w_xq"], kt, vm, w["w_xo"], tm=1024)
        x = _conv_ffn(x, w["g_ffn"], w["w_gate"], w["w_val"], w["fcw"], w["fcb"], w["w_down"], final_g, tm=512,
                      final_norm=(l == depth - 1))
    return x


def kernel(x_prompt, x_sample, mem_prompt, mem_sample, norm_mix_g, w_in, mlstm_conv_w, mlstm_conv_b, mlstm_igate_b, mlstm_fgate_b, mlstm_norm_g, qk_norm_g, w_out, norm_x_g, norm_mem_g, w_xq, w_xkv, w_xo, norm_ffn_g, w_ffn_up, ffn_conv_w, ffn_conv_b, w_ffn_down, final_norm_g):
    p = dict(norm_mix_g=norm_mix_g, w_in=w_in, mlstm_conv_w=mlstm_conv_w, mlstm_conv_b=mlstm_conv_b,
             mlstm_igate_b=mlstm_igate_b, mlstm_fgate_b=mlstm_fgate_b, mlstm_norm_g=mlstm_norm_g,
             qk_norm_g=qk_norm_g, w_out=w_out, norm_x_g=norm_x_g, norm_mem_g=norm_mem_g, w_xq=w_xq, w_xkv=w_xkv,
             w_xo=w_xo, norm_ffn_g=norm_ffn_g, w_ffn_up=w_ffn_up, ffn_conv_w=ffn_conv_w, ffn_conv_b=ffn_conv_b,
             w_ffn_down=w_ffn_down)
    layers = [_prep_layer(p, l) for l in range(w_in.shape[0])]
    tri = np.tril(np.ones((MLSTM_CHUNK, MLSTM_CHUNK), np.float32))
    consts = dict(bd=_block_diag_mean(MXU_TILE),
                  tl=jnp.asarray(tri, BF16), tu=jnp.asarray(tri.T, BF16))
    fg = final_norm_g.reshape(1, D_MODEL)
    return (_trunk(x_prompt, mem_prompt, layers, fg, consts), _trunk(x_sample, mem_sample, layers, fg, consts))
```

```python
import functools

import numpy as np
import jax
import jax.numpy as jnp
from jax import lax
from jax.experimental import pallas as pl
from jax.experimental.pallas import tpu as pltpu

F32 = jnp.float32
BF16 = jnp.bfloat16

LANES = 128
SUBLANES = 8
MXU_TILE = 256
VMEM_LIMIT_BYTES = 56 * 1024 * 1024

D_MODEL = 1024
HEAD_DIM = 64
A_HEADS = 4
B_HEADS = 4
C_HEADS = 8
C_KV_HEADS = 2
A_WIDTH = A_HEADS * HEAD_DIM
B_WIDTH = B_HEADS * HEAD_DIM
C_WIDTH = C_HEADS * HEAD_DIM
C_KV_WIDTH = C_KV_HEADS * HEAD_DIM
DILATED_PAIRS = ((128, 1), (512, 4), (2048, 16))
BAND = 64
ROPE_THETA = 500000.0
ROPE_DIMS = HEAD_DIM // 4
AXIAL_THETA = 10000.0
GRID_W = 64
N_MEM = 256
X_HEADS = 4
X_WIDTH = 256
D_FF = 2752
D_FF_PAD = 2816
RMS_EPS = 1e-6
NEG_INF = -1e30
Q_SCALE = HEAD_DIM ** -0.5
LOG2_E = 1.4426950408889634

COL_A = 0
COL_BQK = COL_A + 3 * A_WIDTH
COL_BV = COL_BQK + 2 * B_WIDTH
COL_BO = COL_BV + B_WIDTH
COL_CQ = COL_BO + B_WIDTH
COL_CK = COL_CQ + C_WIDTH
COL_CV = COL_CK + C_KV_WIDTH
COL_G = COL_CV + C_KV_WIDTH
IN_COLS = COL_G + LANES
N_GATE = 2 * B_HEADS

TQ_DIL = 128
DIL_UNROLL = 8
MLSTM_CHUNK = 256
TQ_GQA = 256
KC_GQA = 1024
FF_CHUNK = 256
OUT_CROSS_BLOCKS = 2


def _dot(a, b):
    return jnp.dot(a, b, preferred_element_type=F32)


def _dot_nt(a, b):
    return lax.dot_general(a, b, (((1,), (1,)), ((), ())), preferred_element_type=F32)


def _split2(x):
    hi = x.astype(BF16)
    return hi, (x - hi.astype(F32)).astype(BF16)


def _split3(x):
    hi = x.astype(BF16)
    r = x - hi.astype(F32)
    mid = r.astype(BF16)
    return hi, mid, (r - mid.astype(F32)).astype(BF16)


def _head_mean(xx, bd):
    w = xx.shape[1]
    tile = bd.shape[0]
    xb = xx.astype(BF16)
    if w <= tile:
        return _dot(xb, bd[:w, :w])
    return jnp.concatenate([_dot(xb[:, t:t + tile], bd) for t in range(0, w, tile)], axis=1)


def _rms(x, g):
    return x * lax.rsqrt(jnp.mean(x * x, axis=-1, keepdims=True) + RMS_EPS) * g


def _lane_iota(shape):
    return lax.broadcasted_iota(jnp.int32, shape, len(shape) - 1)


def _head_mask(width, h, dtype):
    lane = _lane_iota((1, width))
    return ((lane >= h * HEAD_DIM) & (lane < (h + 1) * HEAD_DIM)).astype(dtype)


def _rope(x, tab_ref, shift):
    w = x.shape[1]
    rep = w // LANES
    tile = lambda t: jnp.concatenate([t] * rep, axis=1) if rep > 1 else t
    c, sm, sp = tile(tab_ref[0]), tile(tab_ref[1]), tile(tab_ref[2])
    return x * c + pltpu.roll(x, w - shift, axis=1) * sm + pltpu.roll(x, shift, axis=1) * sp


def _shift_rows(z, prev_row, next_row):
    n = z.shape[0]
    row = lax.broadcasted_iota(jnp.int32, z.shape, 0)
    zp = jnp.where(row == 0, prev_row, pltpu.roll(z, 1, axis=0))
    zn = jnp.where(row == n - 1, next_row, pltpu.roll(z, n - 1, axis=0))
    return zp, zn


def _params(sem):
    return pltpu.CompilerParams(dimension_semantics=sem, vmem_limit_bytes=VMEM_LIMIT_BYTES)


def _lockstep(gens):
    gens = list(gens)
    results = [None] * len(gens)
    live = list(range(len(gens)))
    while live:
        still = []
        for i in live:
            try:
                next(gens[i])
                still.append(i)
            except StopIteration as done:
                results[i] = done.value
        live = still
    return results


def _const_spec(shape):
    n = len(shape)
    return pl.BlockSpec(shape, lambda *_: (0,) * n)


def _mem_kv_kernel(mem_ref, g_ref, w_ref, kt_ref, v_ref):
    hn = _rms(mem_ref[0], g_ref[...]).astype(BF16)
    kv = _dot(hn, w_ref[...])
    kt_ref[0] = kv[:, :X_WIDTH].T.astype(BF16)
    v_ref[0] = kv[:, X_WIDTH:].astype(BF16)


def _mem_kv(mem, g, w):
    b = mem.shape[0]
    return pl.pallas_call(
        _mem_kv_kernel,
        grid=(b,),
        in_specs=[pl.BlockSpec((1, N_MEM, D_MODEL), lambda i: (i, 0, 0)),
                  _const_spec((1, D_MODEL)), _const_spec((D_MODEL, 2 * X_WIDTH))],
        out_specs=[pl.BlockSpec((1, X_WIDTH, N_MEM), lambda i: (i, 0, 0)),
                   pl.BlockSpec((1, N_MEM, X_WIDTH), lambda i: (i, 0, 0))],
        out_shape=[jax.ShapeDtypeStruct((b, X_WIDTH, N_MEM), BF16),
                   jax.ShapeDtypeStruct((b, N_MEM, X_WIDTH), BF16)],
        compiler_params=_params(("arbitrary",)),
        name="mem_kv",
    )(mem, g, w)


def _in_proj_kernel(x_ref, xp_ref, xn_ref, g_ref, w_ref, cw_ref, cb_ref, ks_ref, ta_ref, tc_ref, qg_ref, kg_ref,
                    bd_ref, a1_ref, a4_ref, a16_ref, bqk_ref, bv_ref, bo_ref, bg_ref, cq_ref, ckv_ref, asc_ref):
    i = pl.program_id(1)
    last = pl.num_programs(1) - 1
    g = g_ref[...]
    hn = _rms(x_ref[0], g).astype(BF16)
    tm = hn.shape[0]

    zc = _dot(hn, w_ref[:, COL_CQ:COL_G])
    cq = zc[:, :C_WIDTH]
    cq = cq * lax.rsqrt(_head_mean(cq * cq, bd_ref[...]) + RMS_EPS) * qg_ref[...]
    cq_ref[0] = _rope(cq, tc_ref, HEAD_DIM // 4).astype(BF16)
    ck = zc[:, C_WIDTH:C_WIDTH + C_KV_WIDTH]
    ck = ck * lax.rsqrt(_head_mean(ck * ck, bd_ref[...]) + RMS_EPS) * kg_ref[...]
    ckv_ref[0, :, :C_KV_WIDTH] = _rope(ck, tc_ref, HEAD_DIM // 4).astype(BF16)
    ckv_ref[0, :, C_KV_WIDTH:] = zc[:, C_WIDTH + C_KV_WIDTH:].astype(BF16)

    za = _dot(hn, w_ref[:, COL_A:COL_BQK])
    av = jnp.concatenate([_rope(za[:, :2 * A_WIDTH], ta_ref, ROPE_DIMS // 2), za[:, 2 * A_WIDTH:]], axis=1)
    a1_ref[0, 0] = av.astype(BF16)
    for c in range(3 * A_WIDTH // LANES):
        asc_ref[c] = av[:, c * LANES:(c + 1) * LANES]
    for dil, ref in ((DILATED_PAIRS[1][1], a4_ref), (DILATED_PAIRS[2][1], a16_ref)):
        for r in range(dil):
            for c in range(3 * A_WIDTH // LANES):
                ref[0, r, :, c * LANES:(c + 1) * LANES] = asc_ref[c, pl.ds(r, tm // dil, stride=dil), :].astype(BF16)

    wb = w_ref[:, COL_BQK:COL_BV]
    zb = _dot(hn, wb)
    zprev = _dot(_rms(xp_ref[0], g).astype(BF16), wb)[SUBLANES - 1:SUBLANES] * (i > 0).astype(F32)
    znext = _dot(_rms(xn_ref[0], g).astype(BF16), wb)[0:1] * (i < last).astype(F32)
    zp, zn = _shift_rows(zb, zprev, znext)
    conv = zp * cw_ref[0:1] + zb * cw_ref[1:2] + zn * cw_ref[2:3] + cb_ref[...]
    bqk_ref[0] = (conv * jax.nn.sigmoid(conv) * ks_ref[...]).astype(BF16)
    zrest = _dot(hn, w_ref[:, COL_BV:COL_CQ])
    bv_ref[0] = zrest[:, :B_WIDTH].astype(BF16)
    bo_ref[0] = zrest[:, B_WIDTH:]
    bg_ref[0] = _dot(hn, w_ref[:, COL_G:IN_COLS])


def _in_proj(x, g, w, cw, cb, kscale, ta, tc, qg, kg, bd, tm):
    b, s, _ = x.shape
    nt = s // tm
    hb = tm // SUBLANES
    tok = lambda width: pl.BlockSpec((1, tm, width), lambda bi, i: (bi, i, 0))
    out_widths = (2 * B_WIDTH, B_WIDTH, B_WIDTH, LANES, C_WIDTH, 2 * C_KV_WIDTH)
    out_dtypes = (BF16, BF16, F32, F32, BF16, BF16)
    dils = [d for _, d in DILATED_PAIRS]
    a_specs = [pl.BlockSpec((1, d, tm // d, 3 * A_WIDTH), lambda bi, i: (bi, 0, i, 0)) for d in dils]
    a_shapes = [jax.ShapeDtypeStruct((b, d, s // d, 3 * A_WIDTH), BF16) for d in dils]
    return pl.pallas_call(
        _in_proj_kernel,
        grid=(b, nt),
        in_specs=[tok(D_MODEL),
                  pl.BlockSpec((1, SUBLANES, D_MODEL), lambda bi, i: (bi, jnp.maximum(i * hb - 1, 0), 0)),
                  pl.BlockSpec((1, SUBLANES, D_MODEL), lambda bi, i: (bi, jnp.minimum((i + 1) * hb, s // SUBLANES - 1), 0)),
                  _const_spec((1, D_MODEL)), _const_spec((D_MODEL, IN_COLS)),
                  _const_spec((3, 2 * B_WIDTH)), _const_spec((1, 2 * B_WIDTH)), _const_spec((1, 2 * B_WIDTH)),
                  pl.BlockSpec((3, tm, LANES), lambda bi, i: (0, i, 0)),
                  pl.BlockSpec((3, tm, LANES), lambda bi, i: (0, i, 0)),
                  _const_spec((1, C_WIDTH)), _const_spec((1, C_KV_WIDTH)), _const_spec((MXU_TILE, MXU_TILE))],
        out_specs=a_specs + [tok(wd) for wd in out_widths],
        out_shape=a_shapes + [jax.ShapeDtypeStruct((b, s, wd), dt) for wd, dt in zip(out_widths, out_dtypes)],
        scratch_shapes=[pltpu.VMEM((3 * A_WIDTH // LANES, tm, LANES), F32)],
        compiler_params=_params(("arbitrary", "arbitrary")),
        name="in_proj",
    )(x, x, x, g, w, cw, cb, kscale, ta, tc, qg, kg, bd)


def _dilated_kernel(a_ref, o_ref, l_ref, *, length, n_res, win):
    n_blk = length // TQ_DIL
    masks = [_head_mask(A_WIDTH, h, BF16) for h in range(A_HEADS)]
    lane = _lane_iota((1, A_WIDTH))
    rel = (lax.broadcasted_iota(jnp.int32, (1, TQ_DIL, win), 2)
           - lax.broadcasted_iota(jnp.int32, (1, TQ_DIL, win), 1))

    def pick(parts):
        out = parts[A_HEADS - 1]
        for h in range(A_HEADS - 2, -1, -1):
            out = jnp.where(lane < (h + 1) * HEAD_DIM, parts[h], out)
        return out

    for r in range(n_res):

        def block(jb, carry, r=r):
            if isinstance(jb, int):
                j0 = jb * TQ_DIL
                ws = min(max(j0 - BAND, 0), length - win)
            else:
                j0 = pl.multiple_of(jb * TQ_DIL, TQ_DIL)
                ws = pl.multiple_of(jnp.clip(j0 - BAND, 0, length - win), BAND)
            q = a_ref[0, r, pl.ds(j0, TQ_DIL), :A_WIDTH]
            kw = a_ref[0, r, pl.ds(ws, win), A_WIDTH:2 * A_WIDTH]
            vw = a_ref[0, r, pl.ds(ws, win), 2 * A_WIDTH:]
            q4 = jnp.concatenate([q * m for m in masks], axis=0)
            s = _dot_nt(q4, kw).reshape(A_HEADS, TQ_DIL, win)
            s = jnp.where(jnp.abs(rel + (ws - j0)) <= BAND, s, NEG_INF)
            m = jnp.max(s, axis=-1, keepdims=True)
            p = jnp.exp(s - m)
            l = jnp.sum(p, axis=-1, keepdims=True)
            pn = (p / l).astype(BF16).reshape(A_HEADS * TQ_DIL, win)
            o4 = _dot(pn, vw).reshape(A_HEADS, TQ_DIL, A_WIDTH)
            lse = m + jnp.log(l)
            o_ref[0, r, pl.ds(j0, TQ_DIL), :] = pick([o4[h] for h in range(A_HEADS)])
            l_ref[0, r, pl.ds(j0, TQ_DIL), :] = pick(
                [jnp.broadcast_to(lse[h], (TQ_DIL, A_WIDTH)) for h in range(A_HEADS)])
            return carry

        if n_blk == 1:
            block(0, 0)
        else:
            lax.fori_loop(0, n_blk, block, 0, unroll=min(n_blk, DIL_UNROLL))


def _dilated_branch(a):
    b, dilation, length, _ = a.shape
    n_res = max(1, min(dilation, 2048 // length))
    win = min(length, TQ_DIL + 2 * BAND)
    out_spec = pl.BlockSpec((1, n_res, length, A_WIDTH), lambda bi, ri: (bi, ri, 0, 0))
    return pl.pallas_call(
        functools.partial(_dilated_kernel, length=length, n_res=n_res, win=win),
        grid=(b, dilation // n_res),
        in_specs=[pl.BlockSpec((1, n_res, length, 3 * A_WIDTH), lambda bi, ri: (bi, ri, 0, 0))],
        out_specs=[out_spec, out_spec],
        out_shape=[jax.ShapeDtypeStruct((b, dilation, length, A_WIDTH), F32)] * 2,
        compiler_params=_params(("arbitrary", "arbitrary")),
        name=f"dilated_d{dilation}",
    )(a)


def _cummax_rows(x, reverse):
    n = x.shape[0]
    row = lax.broadcasted_iota(jnp.int32, x.shape, 0)
    step = 1
    while step < n:
        if reverse:
            x = jnp.maximum(x, jnp.where(row < n - step, pltpu.roll(x, n - step, axis=0), NEG_INF))
        else:
            x = jnp.maximum(x, jnp.where(row >= step, pltpu.roll(x, step, axis=0), NEG_INF))
        step *= 2
    return x


def _mlstm_kernel(qk_ref, v_ref, o_ref, g_ref, gb_ref, ng_ref, tl_ref, tu_ref, bd_ref, out_ref,
                  hbuf_ref, c_ref, n_ref, m_ref):
    lc = MLSTM_CHUNK
    n_chunks = qk_ref.shape[1] // lc
    lane = _lane_iota((1, B_WIDTH))
    glane = _lane_iota((1, LANES))
    head_masks = [_head_mask(B_WIDTH, h, BF16) for h in range(B_HEADS)]
    row_i = lax.broadcasted_iota(jnp.int32, (lc, lc), 0)
    col_j = lax.broadcasted_iota(jnp.int32, (lc, lc), 1)
    feat = lax.broadcasted_iota(jnp.int32, (B_WIDTH, B_WIDTH), 0) // HEAD_DIM
    bd_mask = feat == lax.broadcasted_iota(jnp.int32, (B_WIDTH, B_WIDTH), 1) // HEAD_DIM
    nfeat = lax.broadcasted_iota(jnp.int32, (B_WIDTH, LANES), 0) // HEAD_DIM
    nlane = lax.broadcasted_iota(jnp.int32, (B_WIDTH, LANES), 1)

    head_of_lane = lax.broadcasted_iota(jnp.int32, (lc, LANES), 1) // HEAD_DIM

    def spread(cols):
        out = cols[B_HEADS - 1]
        for h in range(B_HEADS - 2, -1, -1):
            out = jnp.where(lane < (h + 1) * HEAD_DIM, cols[h], out)
        return out

    def chunk(c, reverse):
        base = B_HEADS if reverse else 0
        r0 = pl.multiple_of(c * lc, lc)
        q = qk_ref[0, pl.ds(r0, lc), :B_WIDTH]
        k = qk_ref[0, pl.ds(r0, lc), B_WIDTH:]
        v = v_ref[0, pl.ds(r0, lc), :]
        gates = g_ref[0, pl.ds(r0, lc), :] + gb_ref[...]
        ig = gates
        lsig = jnp.minimum(gates, 0.0) - jnp.log1p(jnp.exp(-jnp.abs(gates)))
        lf = pltpu.roll(lsig, LANES - N_GATE, axis=1)
        tri = (tu_ref if reverse else tl_ref)[...]
        hi, mid, lo = _split3(lf)
        bcol = _dot(tri, hi) + _dot(tri, mid) + _dot(tri, lo)
        k_t = k.astype(F32).T.astype(BF16)
        yield
        tot = jnp.sum(lf, axis=0, keepdims=True)
        rcol = ig - bcol
        gend = tot + rcol
        d = int(reverse)
        m_prev = m_ref[d]
        m_new = jnp.maximum(tot + m_prev, jnp.max(gend, axis=0, keepdims=True))
        wend = jnp.exp(gend - m_new)
        decay = jnp.exp(tot + m_prev - m_new)
        r2_t = (rcol * LOG2_E).T
        q_c = _dot(q, c_ref[d].astype(BF16))
        q_n = _dot(q, n_ref[d].astype(BF16))
        yield
        acol = -jnp.maximum(m_prev, _cummax_rows(rcol, reverse))
        wstate = jnp.exp(m_prev + acol)
        floor_ = jnp.exp(acol - bcol)
        a2 = acol * LOG2_E
        mask = (col_j >= row_i) if reverse else (col_j <= row_i)
        ws, sum_w = [], jnp.zeros((lc, LANES), F32)
        for h in range(B_HEADS):
            yield
            dh = base + h
            sc = _dot(q * head_masks[h], k_t)
            e = jnp.exp2(jnp.where(mask, a2[:, dh:dh + 1] + r2_t[dh:dh + 1, :], NEG_INF))
            w = e * sc
            sum_w = jnp.where(glane == dh, jnp.sum(w, axis=1, keepdims=True), sum_w)
            ws.append(w.astype(BF16))
        yield
        den = wstate * q_n + sum_w
        rinv = 1.0 / jnp.maximum(jnp.abs(den), floor_)
        def heads(t):
            if t.shape[0] != lc:
                return spread([t[:, base + h:base + h + 1] for h in range(B_HEADS)])
            return jnp.concatenate(
                [jnp.take_along_axis(t, base + tile * (LANES // HEAD_DIM) + head_of_lane, axis=1)
                 for tile in range(B_WIDTH // LANES)], axis=1)
        pv = _dot(jnp.concatenate(ws, axis=1), jnp.concatenate([v * hm for hm in head_masks], axis=0))
        upd = _dot(k_t, (v.astype(F32) * heads(wend)).astype(BF16))
        nupd = _dot(k_t, wend.astype(BF16))
        yield
        h_out = (heads(wstate) * q_c + pv) * heads(rinv)
        c_ref[d] = heads(decay) * c_ref[d] + jnp.where(bd_mask, upd, 0.0)
        n_ref[d] = decay * n_ref[d] + jnp.where(nfeat == nlane - base, nupd, 0.0)
        m_ref[d] = m_new
        return r0, h_out

    def both(t):
        return _lockstep([chunk(n_chunks - 1 - t if reverse else t, reverse) for reverse in (False, True)])

    def finish(r0, hsum):
        hn = hsum * lax.rsqrt(_head_mean(hsum * hsum, bd_ref[...]) + RMS_EPS) * ng_ref[...]
        out_ref[0, pl.ds(r0, lc), :] = (jax.nn.sigmoid(o_ref[0, pl.ds(r0, lc), :]) * hn).astype(BF16)

    c_ref[...] = jnp.zeros_like(c_ref)
    n_ref[...] = jnp.zeros_like(n_ref)
    m_ref[...] = jnp.zeros_like(m_ref)

    def first_half(t, carry):
        for r0, h_out in both(t):
            hbuf_ref[pl.ds(r0, lc), :] = h_out
        return carry

    def second_half(t, carry):
        for r0, h_out in both(t):
            finish(r0, h_out + hbuf_ref[pl.ds(r0, lc), :])
        return carry

    lax.fori_loop(0, n_chunks // 2, first_half, 0)
    lax.fori_loop(n_chunks // 2, n_chunks, second_half, 0)


def _mlstm(bqk, bv, bo, bg, gbias, ng, tl, tu, bd):
    b, s, _ = bqk.shape
    assert s % (2 * MLSTM_CHUNK) == 0
    row = lambda width: pl.BlockSpec((1, s, width), lambda i: (i, 0, 0))
    return pl.pallas_call(
        _mlstm_kernel,
        grid=(b,),
        in_specs=[row(2 * B_WIDTH), row(B_WIDTH), row(B_WIDTH), row(LANES),
                  _const_spec((1, LANES)), _const_spec((1, B_WIDTH)),
                  _const_spec((MLSTM_CHUNK, MLSTM_CHUNK)), _const_spec((MLSTM_CHUNK, MLSTM_CHUNK)),
                  _const_spec((MXU_TILE, MXU_TILE))],
        out_specs=row(B_WIDTH),
        out_shape=jax.ShapeDtypeStruct((b, s, B_WIDTH), BF16),
        scratch_shapes=[pltpu.VMEM((s, B_WIDTH), F32), pltpu.VMEM((2, B_WIDTH, B_WIDTH), F32),
                        pltpu.VMEM((2, B_WIDTH, LANES), F32), pltpu.VMEM((2, 1, LANES), F32)],
        compiler_params=_params(("arbitrary",)),
        name="mlstm",
    )(bqk, bv, bo, bg, gbias, ng, tl, tu, bd)


def _gqa_kernel(q_ref, kv_ref, o_ref, kt_ref, vx_ref):
    s = kv_ref.shape[1]
    n_kc = s // KC_GQA
    lo = _lane_iota((1, LANES)) < HEAD_DIM
    lo_b = lo.astype(BF16)
    hi_b = 1.0 - lo_b

    @pl.when(pl.program_id(1) == 0)
    def _():
        def prep(c, carry):
            r0 = pl.multiple_of(c * KC_GQA, KC_GQA)
            kc = kv_ref[0, pl.ds(r0, KC_GQA), :C_KV_WIDTH].astype(F32)
            kr = pltpu.roll(kc, HEAD_DIM, axis=1)
            kt_ref[0, c] = jnp.where(lo, kc, kr).T.astype(BF16)
            kt_ref[1, c] = jnp.where(lo, kr, kc).T.astype(BF16)
            vc = kv_ref[0, pl.ds(r0, KC_GQA), C_KV_WIDTH:].astype(F32)
            vr = pltpu.roll(vc, HEAD_DIM, axis=1)
            vx_ref[0, pl.ds(r0, KC_GQA), :] = jnp.where(lo, vc, 0.0).astype(BF16)
            vx_ref[1, pl.ds(r0, KC_GQA), :] = jnp.where(lo, 0.0, vr).astype(BF16)
            vx_ref[2, pl.ds(r0, KC_GQA), :] = jnp.where(lo, vr, 0.0).astype(BF16)
            vx_ref[3, pl.ds(r0, KC_GQA), :] = jnp.where(lo, 0.0, vc).astype(BF16)
            return carry

        lax.fori_loop(0, n_kc, prep, 0)

    tq = q_ref.shape[1]
    per_grp = C_HEADS // C_KV_HEADS
    for grp in range(C_KV_HEADS):
        pairs = [grp * per_grp // 2 + j for j in range(per_grp // 2)]
        qs = []
        for pair in pairs:
            qp = q_ref[0, :, pair * LANES:(pair + 1) * LANES]
            qs += [qp * lo_b, qp * hi_b]

        def body(c, carry, qs=qs, grp=grp):
            ms, ls, accs = carry
            kt = kt_ref[grp, c]
            r0 = pl.multiple_of(c * KC_GQA, KC_GQA)
            new_m, new_l, alphas, ps = [], [], [], []
            for h in range(per_grp):
                sc = _dot(qs[h], kt)
                m_new = jnp.maximum(ms[h], jnp.max(sc, axis=1, keepdims=True))
                alpha = jnp.exp2(ms[h] - m_new)
                p = jnp.exp2(sc - m_new)
                new_m.append(m_new)
                new_l.append(alpha * ls[h] + jnp.sum(p, axis=1, keepdims=True))
                alphas.append(alpha)
                ps.append(p.astype(BF16))
            new_acc = []
            for j in range(per_grp // 2):
                scale = jnp.where(lo, alphas[2 * j], alphas[2 * j + 1])
                new_acc.append(scale * accs[j]
                               + _dot(ps[2 * j], vx_ref[2 * grp, pl.ds(r0, KC_GQA), :])
                               + _dot(ps[2 * j + 1], vx_ref[2 * grp + 1, pl.ds(r0, KC_GQA), :]))
            return tuple(new_m), tuple(new_l), tuple(new_acc)

        init = (tuple(jnp.full((tq, 1), NEG_INF, F32) for _ in range(per_grp)),
                tuple(jnp.zeros((tq, 1), F32) for _ in range(per_grp)),
                tuple(jnp.zeros((tq, LANES), F32) for _ in range(per_grp // 2)))
        _, ls, accs = lax.fori_loop(0, n_kc, body, init, unroll=True)
        for j, pair in enumerate(pairs):
            o_ref[0, :, pair * LANES:(pair + 1) * LANES] = (
                accs[j] / jnp.where(lo, ls[2 * j], ls[2 * j + 1])).astype(BF16)


def _gqa(cq, ckv):
    b, s, _ = cq.shape
    return pl.pallas_call(
        _gqa_kernel,
        grid=(b, s // TQ_GQA),
        in_specs=[pl.BlockSpec((1, TQ_GQA, C_WIDTH), lambda bi, i: (bi, i, 0)),
                  pl.BlockSpec((1, s, 2 * C_KV_WIDTH), lambda bi, i: (bi, 0, 0))],
        out_specs=pl.BlockSpec((1, TQ_GQA, C_WIDTH), lambda bi, i: (bi, i, 0)),
        out_shape=jax.ShapeDtypeStruct((b, s, C_WIDTH), BF16),
        scratch_shapes=[pltpu.VMEM((C_KV_HEADS, s // KC_GQA, LANES, KC_GQA), BF16),
                        pltpu.VMEM((2 * C_KV_HEADS, s, LANES), BF16)],
        compiler_params=_params(("arbitrary", "arbitrary")),
        name="gqa",
    )(cq, ckv)


def _out_cross_kernel(x_ref, o1_ref, l1_ref, o2_ref, l2_ref, o3_ref, l3_ref, yb_ref, yc_ref, wo_ref,
                      gx_ref, wq_ref, kt_ref, vm_ref, wxo_ref, out_ref, il_ref):
    tm = x_ref.shape[1]

    def tokens(ref, slot):
        d = ref.shape[1]
        if d == 1:
            return ref[0, 0]
        tiles = A_WIDTH // LANES
        for r in range(d):
            for c in range(tiles):
                il_ref[slot * tiles + c, pl.ds(r, tm // d, stride=d), :] = ref[0, r, :, c * LANES:(c + 1) * LANES]
        return jnp.concatenate([il_ref[slot * tiles + c] for c in range(tiles)], axis=1)

    l1, l2, l3 = tokens(l1_ref, 0), tokens(l2_ref, 0), tokens(l3_ref, 1)
    o1, o2, o3 = tokens(o1_ref, 0), tokens(o2_ref, 2), tokens(o3_ref, 3)
    lane = _lane_iota((1, X_WIDTH))
    masks = [_head_mask(X_WIDTH, h, BF16) for h in range(X_HEADS)]

    def row_block(rows):
        rb = rows.stop - rows.start
        mx = jnp.maximum(jnp.maximum(l1[rows], l2[rows]), l3[rows])
        e1, e2, e3 = jnp.exp(l1[rows] - mx), jnp.exp(l2[rows] - mx), jnp.exp(l3[rows] - mx)
        ya = (e1 * o1[rows] + e2 * o2[rows] + e3 * o3[rows]) / (e1 + e2 + e3)
        cat = jnp.concatenate([ya.astype(BF16), yb_ref[0, rows, :], yc_ref[0, rows, :]], axis=1)
        x1 = x_ref[0, rows, :] + _dot(cat, wo_ref[...])
        yield
        q = _dot(_rms(x1, gx_ref[...]).astype(BF16), wq_ref[...]).astype(BF16)
        yield
        q4 = jnp.concatenate([q * m for m in masks], axis=0)
        s = _dot(q4, kt_ref[0])
        yield
        m = jnp.max(s, axis=1, keepdims=True)
        p = jnp.exp(s - m)
        pn = (p / jnp.sum(p, axis=1, keepdims=True)).astype(BF16)
        o4 = _dot(pn, vm_ref[0])
        yield
        o = o4[(X_HEADS - 1) * rb:]
        for h in range(X_HEADS - 2, -1, -1):
            o = jnp.where(lane < (h + 1) * HEAD_DIM, o4[h * rb:(h + 1) * rb], o)
        out_ref[0, rows, :] = x1 + _dot(o.astype(BF16), wxo_ref[...])

    rb = tm // OUT_CROSS_BLOCKS
    _lockstep([row_block(slice(r0, r0 + rb)) for r0 in range(0, tm, rb)])


def _out_cross(x, branches, yb, yc, wo, gx, wq, kt, vm, wxo, tm):
    b, s, _ = x.shape
    tok = lambda width: pl.BlockSpec((1, tm, width), lambda bi, i: (bi, i, 0))
    per_b = lambda shape: pl.BlockSpec((1,) + shape, lambda bi, i: (bi, 0, 0))
    flat = [t for pair in branches for t in pair]
    res = lambda t: pl.BlockSpec((1, t.shape[1], tm // t.shape[1], A_WIDTH), lambda bi, i: (bi, 0, i, 0))
    return pl.pallas_call(
        _out_cross_kernel,
        grid=(b, s // tm),
        in_specs=[tok(D_MODEL)] + [res(t) for t in flat] + [tok(B_WIDTH), tok(C_WIDTH),
                  _const_spec((D_MODEL, D_MODEL)), _const_spec((1, D_MODEL)), _const_spec((D_MODEL, X_WIDTH)),
                  per_b((X_WIDTH, N_MEM)), per_b((N_MEM, X_WIDTH)), _const_spec((X_WIDTH, D_MODEL))],
        out_specs=tok(D_MODEL),
        out_shape=jax.ShapeDtypeStruct((b, s, D_MODEL), F32),
        scratch_shapes=[pltpu.VMEM((4 * A_WIDTH // LANES, tm, LANES), F32)],
        compiler_params=_params(("arbitrary", "arbitrary")),
        name="out_cross",
    )(x, *flat, yb, yc, wo, gx, wq, kt, vm, wxo)


def _conv_ffn_kernel(x_ref, xp_ref, xn_ref, g_ref, wg_ref, wv_ref, cw_ref, cb_ref, wd_ref, fg_ref, out_ref, act_ref,
                     *, final_norm):
    i = pl.program_id(1)
    last = pl.num_programs(1) - 1
    g = g_ref[...]
    x = x_ref[0]
    tm = x.shape[0]
    hn = _rms(x, g).astype(BF16)
    hp = (_rms(xp_ref[0], g) * (i > 0).astype(F32)).astype(BF16)
    hx = (_rms(xn_ref[0], g) * (i < last).astype(F32)).astype(BF16)
    h_ext = jnp.concatenate([hp, hn, hx], axis=0)
    for c in range(D_FF_PAD // FF_CHUNK):
        cols = slice(c * FF_CHUNK, (c + 1) * FF_CHUNK)
        zg = _dot(h_ext, wg_ref[:, cols])
        ext = tm + 2 * SUBLANES
        zp = pltpu.roll(zg, 1, axis=0)[SUBLANES:SUBLANES + tm]
        zn = pltpu.roll(zg, ext - 1, axis=0)[SUBLANES:SUBLANES + tm]
        conv = (zp * cw_ref[0:1, cols] + zg[SUBLANES:SUBLANES + tm] * cw_ref[1:2, cols]
                + zn * cw_ref[2:3, cols] + cb_ref[:, cols])
        act_ref[:, cols] = (conv * jax.nn.sigmoid(conv) * _dot(hn, wv_ref[:, cols])).astype(BF16)
    y = x + _dot(act_ref[...], wd_ref[...])
    if final_norm:
        y = _rms(y, fg_ref[...])
    out_ref[0] = y


def _conv_ffn(x, g, wg, wv, cw, cb, wd, fg, tm, final_norm):
    b, s, _ = x.shape
    hb = tm // SUBLANES
    tok = pl.BlockSpec((1, tm, D_MODEL), lambda bi, i: (bi, i, 0))
    once = lambda shape: pl.BlockSpec(shape, lambda *_: (0,) * len(shape), pipeline_mode=pl.Buffered(1))
    return pl.pallas_call(
        functools.partial(_conv_ffn_kernel, final_norm=final_norm),
        grid=(b, s // tm),
        in_specs=[tok,
                  pl.BlockSpec((1, SUBLANES, D_MODEL), lambda bi, i: (bi, jnp.maximum(i * hb - 1, 0), 0)),
                  pl.BlockSpec((1, SUBLANES, D_MODEL), lambda bi, i: (bi, jnp.minimum((i + 1) * hb, s // SUBLANES - 1), 0)),
                  _const_spec((1, D_MODEL)), once((D_MODEL, D_FF_PAD)), once((D_MODEL, D_FF_PAD)),
                  _const_spec((3, D_FF_PAD)), _const_spec((1, D_FF_PAD)), once((D_FF_PAD, D_MODEL)),
                  _const_spec((1, D_MODEL))],
        out_specs=tok,
        out_shape=jax.ShapeDtypeStruct((b, s, D_MODEL), F32),
        scratch_shapes=[pltpu.VMEM((tm, D_FF_PAD), BF16)],
        compiler_params=_params(("arbitrary", "arbitrary")),
        name="conv_ffn",
    )(x, x, x, g, wg, wv, cw, cb, wd, fg)


def _rope_tables(positions, dim, theta):
    half = dim // 2
    s = positions[0].shape[0]
    inv = theta ** (-jnp.arange(0, dim, 2, dtype=F32) / dim)
    zeros = jnp.zeros((s, half), F32)
    c, sm, sp = [], [], []
    for pos in positions:
        ang = pos.astype(F32)[:, None] * inv[None, :]
        cos, sin = jnp.cos(ang), jnp.sin(ang)
        c += [cos, cos]
        sm += [-sin, zeros]
        sp += [zeros, sin]
    rest = HEAD_DIM - dim * len(positions)
    if rest:
        c.append(jnp.ones((s, rest), F32))
        sm.append(jnp.zeros((s, rest), F32))
        sp.append(jnp.zeros((s, rest), F32))
    two = lambda parts: jnp.concatenate(parts + parts, axis=1)
    return jnp.stack([two(c), two(sm), two(sp)], axis=0)


def _block_diag_mean(width):
    blk = np.arange(width) // HEAD_DIM
    return jnp.asarray((blk[:, None] == blk[None, :]).astype(np.float32) / HEAD_DIM, BF16)


def _prep_layer(p, l):
    splits = np.cumsum((A_WIDTH, A_WIDTH, A_WIDTH, 2 * B_WIDTH, B_WIDTH, B_WIDTH, 4 * B_HEADS,
                        C_WIDTH, C_KV_WIDTH, C_KV_WIDTH))[:-1].tolist()
    a_q, a_k, a_v, b_qk, b_v, b_o, b_g, c_q, c_k, c_v = jnp.split(p["w_in"][l], splits, axis=1)
    g4 = b_g.reshape(D_MODEL, 2, 2, B_HEADS)
    gates = jnp.concatenate([g4[:, :, 0].reshape(D_MODEL, N_GATE), g4[:, :, 1].reshape(D_MODEL, N_GATE),
                             jnp.zeros((D_MODEL, LANES - 2 * N_GATE), F32)], axis=1)
    w_in = jnp.concatenate([a_q * Q_SCALE, a_k, a_v, b_qk, b_v, b_o, c_q, c_k, c_v, gates], axis=1).astype(BF16)
    gbias = jnp.concatenate([p["mlstm_igate_b"][l].reshape(N_GATE), p["mlstm_fgate_b"][l].reshape(N_GATE),
                             jnp.zeros((LANES - 2 * N_GATE,), F32)]).reshape(1, LANES)
    kscale = jnp.concatenate([jnp.ones((B_WIDTH,), F32), jnp.full((B_WIDTH,), Q_SCALE, F32)]).reshape(1, 2 * B_WIDTH)
    pad_c = lambda w: jnp.pad(w, ((0, 0), (0, D_FF_PAD - D_FF)))
    w_up = p["w_ffn_up"][l]
    return dict(
        g_mix=p["norm_mix_g"][l].reshape(1, D_MODEL), w_in=w_in,
        cw=p["mlstm_conv_w"][l], cb=p["mlstm_conv_b"][l].reshape(1, 2 * B_WIDTH), kscale=kscale, gbias=gbias,
        ng=p["mlstm_norm_g"][l].reshape(1, B_WIDTH),
        qg=jnp.tile(p["qk_norm_g"][l, 0] * (Q_SCALE * LOG2_E), C_HEADS).reshape(1, C_WIDTH),
        kg=jnp.tile(p["qk_norm_g"][l, 1], C_KV_HEADS).reshape(1, C_KV_WIDTH),
        w_out=p["w_out"][l].astype(BF16),
        g_x=p["norm_x_g"][l].reshape(1, D_MODEL), g_mem=p["norm_mem_g"][l].reshape(1, D_MODEL),
        w_xq=(p["w_xq"][l] * Q_SCALE).astype(BF16), w_xkv=p["w_xkv"][l].astype(BF16), w_xo=p["w_xo"][l].astype(BF16),
        g_ffn=p["norm_ffn_g"][l].reshape(1, D_MODEL),
        w_gate=pad_c(w_up[:, :D_FF]).astype(BF16), w_val=pad_c(w_up[:, D_FF:]).astype(BF16),
        fcw=pad_c(p["ffn_conv_w"][l]), fcb=pad_c(p["ffn_conv_b"][l].reshape(1, D_FF)),
        w_down=jnp.pad(p["w_ffn_down"][l], ((0, D_FF_PAD - D_FF), (0, 0))).astype(BF16),
    )


def _trunk(x, mem, layers, final_g, consts):
    b, s, _ = x.shape
    pos = jnp.arange(s)
    ta = _rope_tables([pos], ROPE_DIMS, ROPE_THETA)
    tc = _rope_tables([pos // GRID_W, pos % GRID_W], HEAD_DIM // 2, AXIAL_THETA)
    depth = len(layers)
    for l, w in enumerate(layers):
        kt, vm = _mem_kv(mem, w["g_mem"], w["w_xkv"])
        a1, a4, a16, bqk, bv, bo, bg, cq, ckv = _in_proj(x, w["g_mix"], w["w_in"], w["cw"], w["cb"], w["kscale"],
                                                         ta, tc, w["qg"], w["kg"], consts["bd"], tm=512)
        branches = [_dilated_branch(a) for a in (a1, a4, a16)]
        yb = _mlstm(bqk, bv, bo, bg, w["gbias"], w["ng"], consts["tl"], consts["tu"], consts["bd"])
        yc = _gqa(cq, ckv)
        x = _out_cross(x, branches, yb, yc, w["w_out"], w["g_x"], w["w_xq"], kt, vm, w["w_xo"], tm=1024)
        x = _conv_ffn(x, w["g_ffn"], w["w_gate"], w["w_val"], w["fcw"], w["fcb"], w["w_down"], final_g, tm=512,
                      final_norm=(l == depth - 1))
    return x


def kernel(x_prompt, x_sample, mem_prompt, mem_sample, norm_mix_g, w_in, mlstm_conv_w, mlstm_conv_b, mlstm_igate_b, mlstm_fgate_b, mlstm_norm_g, qk_norm_g, w_out, norm_x_g, norm_mem_g, w_xq, w_xkv, w_xo, norm_ffn_g, w_ffn_up, ffn_conv_w, ffn_conv_b, w_ffn_down, final_norm_g):
    p = dict(norm_mix_g=norm_mix_g, w_in=w_in, mlstm_conv_w=mlstm_conv_w, mlstm_conv_b=mlstm_conv_b,
             mlstm_igate_b=mlstm_igate_b, mlstm_fgate_b=mlstm_fgate_b, mlstm_norm_g=mlstm_norm_g,
             qk_norm_g=qk_norm_g, w_out=w_out, norm_x_g=norm_x_g, norm_mem_g=norm_mem_g, w_xq=w_xq, w_xkv=w_xkv,
             w_xo=w_xo, norm_ffn_g=norm_ffn_g, w_ffn_up=w_ffn_up, ffn_conv_w=ffn_conv_w, ffn_conv_b=ffn_conv_b,
             w_ffn_down=w_ffn_down)
    layers = [_prep_layer(p, l) for l in range(w_in.shape[0])]
    tri = np.tril(np.ones((MLSTM_CHUNK, MLSTM_CHUNK), np.float32))
    consts = dict(bd=_block_diag_mean(MXU_TILE),
                  tl=jnp.asarray(tri, BF16), tu=jnp.asarray(tri.T, BF16))
    fg = final_norm_g.reshape(1, D_MODEL)
    return (_trunk(x_prompt, mem_prompt, layers, fg, consts), _trunk(x_sample, mem_sample, layers, fg, consts))
```

```python
import functools

import numpy as np
import jax
import jax.numpy as jnp
from jax import lax
from jax.experimental import pallas as pl
from jax.experimental.pallas import tpu as pltpu

F32 = jnp.float32
BF16 = jnp.bfloat16

LANES = 128
SUBLANES = 8
MXU_TILE = 256
VMEM_LIMIT_BYTES = 56 * 1024 * 1024

D_MODEL = 1024
HEAD_DIM = 64
A_HEADS = 4
B_HEADS = 4
C_HEADS = 8
C_KV_HEADS = 2
A_WIDTH = A_HEADS * HEAD_DIM
B_WIDTH = B_HEADS * HEAD_DIM
C_WIDTH = C_HEADS * HEAD_DIM
C_KV_WIDTH = C_KV_HEADS * HEAD_DIM
DILATED_PAIRS = ((128, 1), (512, 4), (2048, 16))
BAND = 64
ROPE_THETA = 500000.0
ROPE_DIMS = HEAD_DIM // 4
AXIAL_THETA = 10000.0
GRID_W = 64
N_MEM = 256
X_HEADS = 4
X_WIDTH = 256
D_FF = 2752
D_FF_PAD = 2816
RMS_EPS = 1e-6
NEG_INF = -1e30
Q_SCALE = HEAD_DIM ** -0.5
LOG2_E = 1.4426950408889634

COL_A = 0
COL_BQK = COL_A + 3 * A_WIDTH
COL_BV = COL_BQK + 2 * B_WIDTH
COL_BO = COL_BV + B_WIDTH
COL_CQ = COL_BO + B_WIDTH
COL_CK = COL_CQ + C_WIDTH
COL_CV = COL_CK + C_KV_WIDTH
COL_G = COL_CV + C_KV_WIDTH
IN_COLS = COL_G + LANES
N_GATE = 2 * B_HEADS

TQ_DIL = 128
DIL_UNROLL = 8
MLSTM_CHUNK = 256
TQ_GQA = 256
KC_GQA = 1024
FF_CHUNK = 256
OUT_CROSS_BLOCKS = 4


def _dot(a, b):
    return jnp.dot(a, b, preferred_element_type=F32)


def _dot_nt(a, b):
    return lax.dot_general(a, b, (((1,), (1,)), ((), ())), preferred_element_type=F32)


def _split2(x):
    hi = x.astype(BF16)
    return hi, (x - hi.astype(F32)).astype(BF16)


def _split3(x):
    hi = x.astype(BF16)
    r = x - hi.astype(F32)
    mid = r.astype(BF16)
    return hi, mid, (r - mid.astype(F32)).astype(BF16)


def _head_mean(xx, bd):
    w = xx.shape[1]
    tile = bd.shape[0]
    xb = xx.astype(BF16)
    if w <= tile:
        return _dot(xb, bd[:w, :w])
    return jnp.concatenate([_dot(xb[:, t:t + tile], bd) for t in range(0, w, tile)], axis=1)


def _rms(x, g):
    return x * lax.rsqrt(jnp.mean(x * x, axis=-1, keepdims=True) + RMS_EPS) * g


def _lane_iota(shape):
    return lax.broadcasted_iota(jnp.int32, shape, len(shape) - 1)


def _head_mask(width, h, dtype):
    lane = _lane_iota((1, width))
    return ((lane >= h * HEAD_DIM) & (lane < (h + 1) * HEAD_DIM)).astype(dtype)


def _rope(x, tab_ref, shift):
    w = x.shape[1]
    rep = w // LANES
    tile = lambda t: jnp.concatenate([t] * rep, axis=1) if rep > 1 else t
    c, sm, sp = tile(tab_ref[0]), tile(tab_ref[1]), tile(tab_ref[2])
    return x * c + pltpu.roll(x, w - shift, axis=1) * sm + pltpu.roll(x, shift, axis=1) * sp


def _shift_rows(z, prev_row, next_row):
    n = z.shape[0]
    row = lax.broadcasted_iota(jnp.int32, z.shape, 0)
    zp = jnp.where(row == 0, prev_row, pltpu.roll(z, 1, axis=0))
    zn = jnp.where(row == n - 1, next_row, pltpu.roll(z, n - 1, axis=0))
    return zp, zn


def _params(sem):
    return pltpu.CompilerParams(dimension_semantics=sem, vmem_limit_bytes=VMEM_LIMIT_BYTES)


def _lockstep(gens):
    gens = list(gens)
    results = [None] * len(gens)
    live = list(range(len(gens)))
    while live:
        still = []
        for i in live:
            try:
                next(gens[i])
                still.append(i)
            except StopIteration as done:
                results[i] = done.value
        live = still
    return results


def _const_spec(shape):
    n = len(shape)
    return pl.BlockSpec(shape, lambda *_: (0,) * n)


def _mem_kv_kernel(mem_ref, g_ref, w_ref, kt_ref, v_ref):
    hn = _rms(mem_ref[0], g_ref[...]).astype(BF16)
    kv = _dot(hn, w_ref[...])
    kt_ref[0] = kv[:, :X_WIDTH].T.astype(BF16)
    v_ref[0] = kv[:, X_WIDTH:].astype(BF16)


def _mem_kv(mem, g, w):
    b = mem.shape[0]
    return pl.pallas_call(
        _mem_kv_kernel,
        grid=(b,),
        in_specs=[pl.BlockSpec((1, N_MEM, D_MODEL), lambda i: (i, 0, 0)),
                  _const_spec((1, D_MODEL)), _const_spec((D_MODEL, 2 * X_WIDTH))],
        out_specs=[pl.BlockSpec((1, X_WIDTH, N_MEM), lambda i: (i, 0, 0)),
                   pl.BlockSpec((1, N_MEM, X_WIDTH), lambda i: (i, 0, 0))],
        out_shape=[jax.ShapeDtypeStruct((b, X_WIDTH, N_MEM), BF16),
                   jax.ShapeDtypeStruct((b, N_MEM, X_WIDTH), BF16)],
        compiler_params=_params(("arbitrary",)),
        name="mem_kv",
    )(mem, g, w)


def _in_proj_kernel(x_ref, xp_ref, xn_ref, g_ref, w_ref, cw_ref, cb_ref, ks_ref, ta_ref, tc_ref, qg_ref, kg_ref,
                    bd_ref, a1_ref, a4_ref, a16_ref, bqk_ref, bv_ref, bo_ref, bg_ref, cq_ref, ckv_ref, asc_ref):
    i = pl.program_id(1)
    last = pl.num_programs(1) - 1
    g = g_ref[...]
    hn = _rms(x_ref[0], g).astype(BF16)
    tm = hn.shape[0]

    zc = _dot(hn, w_ref[:, COL_CQ:COL_G])
    cq = zc[:, :C_WIDTH]
    cq = cq * lax.rsqrt(_head_mean(cq * cq, bd_ref[...]) + RMS_EPS) * qg_ref[...]
    cq_ref[0] = _rope(cq, tc_ref, HEAD_DIM // 4).astype(BF16)
    ck = zc[:, C_WIDTH:C_WIDTH + C_KV_WIDTH]
    ck = ck * lax.rsqrt(_head_mean(ck * ck, bd_ref[...]) + RMS_EPS) * kg_ref[...]
    ckv_ref[0, :, :C_KV_WIDTH] = _rope(ck, tc_ref, HEAD_DIM // 4).astype(BF16)
    ckv_ref[0, :, C_KV_WIDTH:] = zc[:, C_WIDTH + C_KV_WIDTH:].astype(BF16)

    za = _dot(hn, w_ref[:, COL_A:COL_BQK])
    av = jnp.concatenate([_rope(za[:, :2 * A_WIDTH], ta_ref, ROPE_DIMS // 2), za[:, 2 * A_WIDTH:]], axis=1)
    a1_ref[0, 0] = av.astype(BF16)
    for c in range(3 * A_WIDTH // LANES):
        asc_ref[c] = av[:, c * LANES:(c + 1) * LANES]
    for dil, ref in ((DILATED_PAIRS[1][1], a4_ref), (DILATED_PAIRS[2][1], a16_ref)):
        for r in range(dil):
            for c in range(3 * A_WIDTH // LANES):
                ref[0, r, :, c * LANES:(c + 1) * LANES] = asc_ref[c, pl.ds(r, tm // dil, stride=dil), :].astype(BF16)

    wb = w_ref[:, COL_BQK:COL_BV]
    zb = _dot(hn, wb)
    zprev = _dot(_rms(xp_ref[0], g).astype(BF16), wb)[SUBLANES - 1:SUBLANES] * (i > 0).astype(F32)
    znext = _dot(_rms(xn_ref[0], g).astype(BF16), wb)[0:1] * (i < last).astype(F32)
    zp, zn = _shift_rows(zb, zprev, znext)
    conv = zp * cw_ref[0:1] + zb * cw_ref[1:2] + zn * cw_ref[2:3] + cb_ref[...]
    bqk_ref[0] = (conv * jax.nn.sigmoid(conv) * ks_ref[...]).astype(BF16)
    zrest = _dot(hn, w_ref[:, COL_BV:COL_CQ])
    bv_ref[0] = zrest[:, :B_WIDTH].astype(BF16)
    bo_ref[0] = zrest[:, B_WIDTH:]
    bg_ref[0] = _dot(hn, w_ref[:, COL_G:IN_COLS])


def _in_proj(x, g, w, cw, cb, kscale, ta, tc, qg, kg, bd, tm):
    b, s, _ = x.shape
    nt = s // tm
    hb = tm // SUBLANES
    tok = lambda width: pl.BlockSpec((1, tm, width), lambda bi, i: (bi, i, 0))
    out_widths = (2 * B_WIDTH, B_WIDTH, B_WIDTH, LANES, C_WIDTH, 2 * C_KV_WIDTH)
    out_dtypes = (BF16, BF16, F32, F32, BF16, BF16)
    dils = [d for _, d in DILATED_PAIRS]
    a_specs = [pl.BlockSpec((1, d, tm // d, 3 * A_WIDTH), lambda bi, i: (bi, 0, i, 0)) for d in dils]
    a_shapes = [jax.ShapeDtypeStruct((b, d, s // d, 3 * A_WIDTH), BF16) for d in dils]
    return pl.pallas_call(
        _in_proj_kernel,
        grid=(b, nt),
        in_specs=[tok(D_MODEL),
                  pl.BlockSpec((1, SUBLANES, D_MODEL), lambda bi, i: (bi, jnp.maximum(i * hb - 1, 0), 0)),
                  pl.BlockSpec((1, SUBLANES, D_MODEL), lambda bi, i: (bi, jnp.minimum((i + 1) * hb, s // SUBLANES - 1), 0)),
                  _const_spec((1, D_MODEL)), _const_spec((D_MODEL, IN_COLS)),
                  _const_spec((3, 2 * B_WIDTH)), _const_spec((1, 2 * B_WIDTH)), _const_spec((1, 2 * B_WIDTH)),
                  pl.BlockSpec((3, tm, LANES), lambda bi, i: (0, i, 0)),
                  pl.BlockSpec((3, tm, LANES), lambda bi, i: (0, i, 0)),
                  _const_spec((1, C_WIDTH)), _const_spec((1, C_KV_WIDTH)), _const_spec((MXU_TILE, MXU_TILE))],
        out_specs=a_specs + [tok(wd) for wd in out_widths],
        out_shape=a_shapes + [jax.ShapeDtypeStruct((b, s, wd), dt) for wd, dt in zip(out_widths, out_dtypes)],
        scratch_shapes=[pltpu.VMEM((3 * A_WIDTH // LANES, tm, LANES), F32)],
        compiler_params=_params(("arbitrary", "arbitrary")),
        name="in_proj",
    )(x, x, x, g, w, cw, cb, kscale, ta, tc, qg, kg, bd)


def _dilated_kernel(a_ref, o_ref, l_ref, *, length, n_res, win):
    n_blk = length // TQ_DIL
    masks = [_head_mask(A_WIDTH, h, BF16) for h in range(A_HEADS)]
    lane = _lane_iota((1, A_WIDTH))
    rel = (lax.broadcasted_iota(jnp.int32, (1, TQ_DIL, win), 2)
           - lax.broadcasted_iota(jnp.int32, (1, TQ_DIL, win), 1))

    def pick(parts):
        out = parts[A_HEADS - 1]
        for h in range(A_HEADS - 2, -1, -1):
            out = jnp.where(lane < (h + 1) * HEAD_DIM, parts[h], out)
        return out

    for r in range(n_res):

        def block(jb, carry, r=r):
            if isinstance(jb, int):
                j0 = jb * TQ_DIL
                ws = min(max(j0 - BAND, 0), length - win)
            else:
                j0 = pl.multiple_of(jb * TQ_DIL, TQ_DIL)
                ws = pl.multiple_of(jnp.clip(j0 - BAND, 0, length - win), BAND)
            q = a_ref[0, r, pl.ds(j0, TQ_DIL), :A_WIDTH]
            kw = a_ref[0, r, pl.ds(ws, win), A_WIDTH:2 * A_WIDTH]
            vw = a_ref[0, r, pl.ds(ws, win), 2 * A_WIDTH:]
            q4 = jnp.concatenate([q * m for m in masks], axis=0)
            s = _dot_nt(q4, kw).reshape(A_HEADS, TQ_DIL, win)
            s = jnp.where(jnp.abs(rel + (ws - j0)) <= BAND, s, NEG_INF)
            m = jnp.max(s, axis=-1, keepdims=True)
            p = jnp.exp2(s - m)
            l = jnp.sum(p, axis=-1, keepdims=True)
            pn = (p * (1.0 / l)).astype(BF16).reshape(A_HEADS * TQ_DIL, win)
            o4 = _dot(pn, vw).reshape(A_HEADS, TQ_DIL, A_WIDTH)
            lse = m + jnp.log2(l)
            o_ref[0, r, pl.ds(j0, TQ_DIL), :] = pick([o4[h] for h in range(A_HEADS)])
            l_ref[0, r, pl.ds(j0, TQ_DIL), :] = pick(
                [jnp.broadcast_to(lse[h], (TQ_DIL, A_WIDTH)) for h in range(A_HEADS)])
            return carry

        if n_blk == 1:
            block(0, 0)
        else:
            lax.fori_loop(0, n_blk, block, 0, unroll=min(n_blk, DIL_UNROLL))


def _dilated_branch(a):
    b, dilation, length, _ = a.shape
    n_res = max(1, min(dilation, 2048 // length))
    win = min(length, TQ_DIL + 2 * BAND)
    out_spec = pl.BlockSpec((1, n_res, length, A_WIDTH), lambda bi, ri: (bi, ri, 0, 0))
    return pl.pallas_call(
        functools.partial(_dilated_kernel, length=length, n_res=n_res, win=win),
        grid=(b, dilation // n_res),
        in_specs=[pl.BlockSpec((1, n_res, length, 3 * A_WIDTH), lambda bi, ri: (bi, ri, 0, 0))],
        out_specs=[out_spec, out_spec],
        out_shape=[jax.ShapeDtypeStruct((b, dilation, length, A_WIDTH), F32)] * 2,
        compiler_params=_params(("arbitrary", "arbitrary")),
        name=f"dilated_d{dilation}",
    )(a)


def _cummax_rows(x, reverse):
    n = x.shape[0]
    row = lax.broadcasted_iota(jnp.int32, x.shape, 0)
    step = 1
    while step < n:
        if reverse:
            x = jnp.maximum(x, jnp.where(row < n - step, pltpu.roll(x, n - step, axis=0), NEG_INF))
        else:
            x = jnp.maximum(x, jnp.where(row >= step, pltpu.roll(x, step, axis=0), NEG_INF))
        step *= 2
    return x


def _mlstm_kernel(qk_ref, v_ref, o_ref, g_ref, gb_ref, ng_ref, tl_ref, tu_ref, bd_ref, out_ref,
                  hbuf_ref, c_ref, n_ref, m_ref):
    lc = MLSTM_CHUNK
    n_chunks = qk_ref.shape[1] // lc
    lane = _lane_iota((1, B_WIDTH))
    glane = _lane_iota((1, LANES))
    head_masks = [_head_mask(B_WIDTH, h, BF16) for h in range(B_HEADS)]
    row_i = lax.broadcasted_iota(jnp.int32, (lc, lc), 0)
    col_j = lax.broadcasted_iota(jnp.int32, (lc, lc), 1)
    feat = lax.broadcasted_iota(jnp.int32, (B_WIDTH, B_WIDTH), 0) // HEAD_DIM
    bd_mask = feat == lax.broadcasted_iota(jnp.int32, (B_WIDTH, B_WIDTH), 1) // HEAD_DIM
    nfeat = lax.broadcasted_iota(jnp.int32, (B_WIDTH, LANES), 0) // HEAD_DIM
    nlane = lax.broadcasted_iota(jnp.int32, (B_WIDTH, LANES), 1)

    head_of_lane = lax.broadcasted_iota(jnp.int32, (lc, LANES), 1) // HEAD_DIM

    def spread(cols):
        out = cols[B_HEADS - 1]
        for h in range(B_HEADS - 2, -1, -1):
            out = jnp.where(lane < (h + 1) * HEAD_DIM, cols[h], out)
        return out

    def chunk(c, reverse):
        base = B_HEADS if reverse else 0
        r0 = pl.multiple_of(c * lc, lc)
        q = qk_ref[0, pl.ds(r0, lc), :B_WIDTH]
        k = qk_ref[0, pl.ds(r0, lc), B_WIDTH:]
        v = v_ref[0, pl.ds(r0, lc), :]
        gates = g_ref[0, pl.ds(r0, lc), :] + gb_ref[...]
        ig = gates
        lsig = jnp.minimum(gates, 0.0) - jnp.log1p(jnp.exp(-jnp.abs(gates)))
        lf = pltpu.roll(lsig, LANES - N_GATE, axis=1)
        tri = (tu_ref if reverse else tl_ref)[...]
        hi, mid, lo = _split3(lf)
        bcol = _dot(tri, hi) + _dot(tri, mid) + _dot(tri, lo)
        k_t = k.astype(F32).T.astype(BF16)
        yield
        tot = jnp.sum(lf, axis=0, keepdims=True)
        rcol = ig - bcol
        gend = tot + rcol
        d = int(reverse)
        m_prev = m_ref[d]
        m_new = jnp.maximum(tot + m_prev, jnp.max(gend, axis=0, keepdims=True))
        wend = jnp.exp(gend - m_new)
        decay = jnp.exp(tot + m_prev - m_new)
        r2_t = (rcol * LOG2_E).T
        q_c = _dot(q, c_ref[d].astype(BF16))
        q_n = _dot(q, n_ref[d].astype(BF16))
        yield
        acol = -jnp.maximum(m_prev, _cummax_rows(rcol, reverse))
        wstate = jnp.exp(m_prev + acol)
        floor_ = jnp.exp(acol - bcol)
        a2 = acol * LOG2_E
        mask = (col_j >= row_i) if reverse else (col_j <= row_i)
        ws, sum_w = [], jnp.zeros((lc, LANES), F32)
        for h in range(B_HEADS):
            yield
            dh = base + h
            sc = _dot(q * head_masks[h], k_t)
            e = jnp.exp2(jnp.where(mask, a2[:, dh:dh + 1] + r2_t[dh:dh + 1, :], NEG_INF))
            yield
            w = e * sc
            sum_w = jnp.where(glane == dh, jnp.sum(w, axis=1, keepdims=True), sum_w)
            ws.append(w.astype(BF16))
        yield
        den = wstate * q_n + sum_w
        rinv = 1.0 / jnp.maximum(jnp.abs(den), floor_)
        def heads(t):
            if t.shape[0] != lc:
                return spread([t[:, base + h:base + h + 1] for h in range(B_HEADS)])
            return jnp.concatenate(
                [jnp.take_along_axis(t, base + tile * (LANES // HEAD_DIM) + head_of_lane, axis=1)
                 for tile in range(B_WIDTH // LANES)], axis=1)
        pv = _dot(jnp.concatenate(ws, axis=1), jnp.concatenate([v * hm for hm in head_masks], axis=0))
        upd = _dot(k_t, (v.astype(F32) * heads(wend)).astype(BF16))
        nupd = _dot(k_t, wend.astype(BF16))
        yield
        h_out = (heads(wstate) * q_c + pv) * heads(rinv)
        c_ref[d] = heads(decay) * c_ref[d] + jnp.where(bd_mask, upd, 0.0)
        n_ref[d] = decay * n_ref[d] + jnp.where(nfeat == nlane - base, nupd, 0.0)
        m_ref[d] = m_new
        return r0, h_out

    def both(t):
        return _lockstep([chunk(n_chunks - 1 - t if reverse else t, reverse) for reverse in (False, True)])

    def finish(r0, hsum):
        hn = hsum * lax.rsqrt(_head_mean(hsum * hsum, bd_ref[...]) + RMS_EPS) * ng_ref[...]
        out_ref[0, pl.ds(r0, lc), :] = (jax.nn.sigmoid(o_ref[0, pl.ds(r0, lc), :]) * hn).astype(BF16)

    c_ref[...] = jnp.zeros_like(c_ref)
    n_ref[...] = jnp.zeros_like(n_ref)
    m_ref[...] = jnp.zeros_like(m_ref)

    def first_half(t, carry):
        for r0, h_out in both(t):
            hbuf_ref[pl.ds(r0, lc), :] = h_out
        return carry

    def second_half(t, carry):
        for r0, h_out in both(t):
            finish(r0, h_out + hbuf_ref[pl.ds(r0, lc), :])
        return carry

    lax.fori_loop(0, n_chunks // 2, first_half, 0)
    lax.fori_loop(n_chunks // 2, n_chunks, second_half, 0)


def _mlstm(bqk, bv, bo, bg, gbias, ng, tl, tu, bd):
    b, s, _ = bqk.shape
    assert s % (2 * MLSTM_CHUNK) == 0
    row = lambda width: pl.BlockSpec((1, s, width), lambda i: (i, 0, 0))
    return pl.pallas_call(
        _mlstm_kernel,
        grid=(b,),
        in_specs=[row(2 * B_WIDTH), row(B_WIDTH), row(B_WIDTH), row(LANES),
                  _const_spec((1, LANES)), _const_spec((1, B_WIDTH)),
                  _const_spec((MLSTM_CHUNK, MLSTM_CHUNK)), _const_spec((MLSTM_CHUNK, MLSTM_CHUNK)),
                  _const_spec((MXU_TILE, MXU_TILE))],
        out_specs=row(B_WIDTH),
        out_shape=jax.ShapeDtypeStruct((b, s, B_WIDTH), BF16),
        scratch_shapes=[pltpu.VMEM((s, B_WIDTH), F32), pltpu.VMEM((2, B_WIDTH, B_WIDTH), F32),
                        pltpu.VMEM((2, B_WIDTH, LANES), F32), pltpu.VMEM((2, 1, LANES), F32)],
        compiler_params=_params(("arbitrary",)),
        name="mlstm",
    )(bqk, bv, bo, bg, gbias, ng, tl, tu, bd)


def _gqa_kernel(q_ref, kv_ref, o_ref, kt_ref, vx_ref):
    s = kv_ref.shape[1]
    n_kc = s // KC_GQA
    lo = _lane_iota((1, LANES)) < HEAD_DIM
    lo_b = lo.astype(BF16)
    hi_b = 1.0 - lo_b

    @pl.when(pl.program_id(1) == 0)
    def _():
        def prep(c, carry):
            r0 = pl.multiple_of(c * KC_GQA, KC_GQA)
            kc = kv_ref[0, pl.ds(r0, KC_GQA), :C_KV_WIDTH].astype(F32)
            kr = pltpu.roll(kc, HEAD_DIM, axis=1)
            kt_ref[0, c] = jnp.where(lo, kc, kr).T.astype(BF16)
            kt_ref[1, c] = jnp.where(lo, kr, kc).T.astype(BF16)
            vc = kv_ref[0, pl.ds(r0, KC_GQA), C_KV_WIDTH:].astype(F32)
            vr = pltpu.roll(vc, HEAD_DIM, axis=1)
            vx_ref[0, pl.ds(r0, KC_GQA), :] = jnp.where(lo, vc, 0.0).astype(BF16)
            vx_ref[1, pl.ds(r0, KC_GQA), :] = jnp.where(lo, 0.0, vr).astype(BF16)
            vx_ref[2, pl.ds(r0, KC_GQA), :] = jnp.where(lo, vr, 0.0).astype(BF16)
            vx_ref[3, pl.ds(r0, KC_GQA), :] = jnp.where(lo, 0.0, vc).astype(BF16)
            return carry

        lax.fori_loop(0, n_kc, prep, 0)

    tq = q_ref.shape[1]
    per_grp = C_HEADS // C_KV_HEADS
    for grp in range(C_KV_HEADS):
        pairs = [grp * per_grp // 2 + j for j in range(per_grp // 2)]
        qs = []
        for pair in pairs:
            qp = q_ref[0, :, pair * LANES:(pair + 1) * LANES]
            qs += [qp * lo_b, qp * hi_b]

        def body(c, carry, qs=qs, grp=grp):
            ms, ls, accs = carry
            kt = kt_ref[grp, c]
            r0 = pl.multiple_of(c * KC_GQA, KC_GQA)
            new_m, new_l, alphas, ps, new_acc = [], [], [], [], []
            sc_next = _dot(qs[0], kt)
            for h in range(per_grp):
                sc = sc_next
                if h + 1 < per_grp:
                    sc_next = _dot(qs[h + 1], kt)
                m_new = jnp.maximum(ms[h], jnp.max(sc, axis=1, keepdims=True))
                alpha = jnp.exp2(ms[h] - m_new)
                p = jnp.exp2(sc - m_new)
                new_m.append(m_new)
                new_l.append(alpha * ls[h] + jnp.sum(p, axis=1, keepdims=True))
                alphas.append(alpha)
                ps.append(p.astype(BF16))
                if h % 2 == 1:
                    j = h // 2
                    scale = jnp.where(lo, alphas[2 * j], alphas[2 * j + 1])
                    new_acc.append(scale * accs[j]
                                   + _dot(ps[2 * j], vx_ref[2 * grp, pl.ds(r0, KC_GQA), :])
                                   + _dot(ps[2 * j + 1], vx_ref[2 * grp + 1, pl.ds(r0, KC_GQA), :]))
            return tuple(new_m), tuple(new_l), tuple(new_acc)

        init = (tuple(jnp.full((tq, 1), NEG_INF, F32) for _ in range(per_grp)),
                tuple(jnp.zeros((tq, 1), F32) for _ in range(per_grp)),
                tuple(jnp.zeros((tq, LANES), F32) for _ in range(per_grp // 2)))
        _, ls, accs = lax.fori_loop(0, n_kc, body, init, unroll=True)
        for j, pair in enumerate(pairs):
            o_ref[0, :, pair * LANES:(pair + 1) * LANES] = (
                accs[j] / jnp.where(lo, ls[2 * j], ls[2 * j + 1])).astype(BF16)


def _gqa(cq, ckv):
    b, s, _ = cq.shape
    return pl.pallas_call(
        _gqa_kernel,
        grid=(b, s // TQ_GQA),
        in_specs=[pl.BlockSpec((1, TQ_GQA, C_WIDTH), lambda bi, i: (bi, i, 0)),
                  pl.BlockSpec((1, s, 2 * C_KV_WIDTH), lambda bi, i: (bi, 0, 0))],
        out_specs=pl.BlockSpec((1, TQ_GQA, C_WIDTH), lambda bi, i: (bi, i, 0)),
        out_shape=jax.ShapeDtypeStruct((b, s, C_WIDTH), BF16),
        scratch_shapes=[pltpu.VMEM((C_KV_HEADS, s // KC_GQA, LANES, KC_GQA), BF16),
                        pltpu.VMEM((2 * C_KV_HEADS, s, LANES), BF16)],
        compiler_params=_params(("arbitrary", "arbitrary")),
        name="gqa",
    )(cq, ckv)


def _out_cross_kernel(x_ref, o1_ref, l1_ref, o2_ref, l2_ref, o3_ref, l3_ref, yb_ref, yc_ref, wo_ref,
                      gx_ref, wq_ref, kt_ref, vm_ref, wxo_ref, out_ref, il_ref):
    tm = x_ref.shape[1]

    def tokens(ref, slot):
        d = ref.shape[1]
        if d == 1:
            return ref[0, 0]
        tiles = A_WIDTH // LANES
        for r in range(d):
            for c in range(tiles):
                il_ref[slot * tiles + c, pl.ds(r, tm // d, stride=d), :] = ref[0, r, :, c * LANES:(c + 1) * LANES]
        return jnp.concatenate([il_ref[slot * tiles + c] for c in range(tiles)], axis=1)

    l1, l2, l3 = tokens(l1_ref, 0), tokens(l2_ref, 0), tokens(l3_ref, 1)
    o1, o2, o3 = tokens(o1_ref, 0), tokens(o2_ref, 2), tokens(o3_ref, 3)
    lane = _lane_iota((1, X_WIDTH))
    masks = [_head_mask(X_WIDTH, h, BF16) for h in range(X_HEADS)]

    def row_block(rows):
        rb = rows.stop - rows.start
        mx = jnp.maximum(jnp.maximum(l1[rows], l2[rows]), l3[rows])
        e1, e2, e3 = jnp.exp2(l1[rows] - mx), jnp.exp2(l2[rows] - mx), jnp.exp2(l3[rows] - mx)
        ya = (e1 * o1[rows] + e2 * o2[rows] + e3 * o3[rows]) / (e1 + e2 + e3)
        cat = jnp.concatenate([ya.astype(BF16), yb_ref[0, rows, :], yc_ref[0, rows, :]], axis=1)
        x1 = x_ref[0, rows, :] + _dot(cat, wo_ref[...])
        yield
        q = _dot(_rms(x1, gx_ref[...]).astype(BF16), wq_ref[...]).astype(BF16)
        yield
        q4 = jnp.concatenate([q * m for m in masks], axis=0)
        s = _dot(q4, kt_ref[0])
        yield
        m = jnp.max(s, axis=1, keepdims=True)
        p = jnp.exp2(s - m)
        pn = (p * (1.0 / jnp.sum(p, axis=1, keepdims=True))).astype(BF16)
        o4 = _dot(pn, vm_ref[0])
        yield
        o = o4[(X_HEADS - 1) * rb:]
        for h in range(X_HEADS - 2, -1, -1):
            o = jnp.where(lane < (h + 1) * HEAD_DIM, o4[h * rb:(h + 1) * rb], o)
        out_ref[0, rows, :] = x1 + _dot(o.astype(BF16), wxo_ref[...])

    rb = tm // OUT_CROSS_BLOCKS
    _lockstep([row_block(slice(r0, r0 + rb)) for r0 in range(0, tm, rb)])


def _out_cross(x, branches, yb, yc, wo, gx, wq, kt, vm, wxo, tm):
    b, s, _ = x.shape
    tok = lambda width: pl.BlockSpec((1, tm, width), lambda bi, i: (bi, i, 0))
    per_b = lambda shape: pl.BlockSpec((1,) + shape, lambda bi, i: (bi, 0, 0))
    flat = [t for pair in branches for t in pair]
    res = lambda t: pl.BlockSpec((1, t.shape[1], tm // t.shape[1], A_WIDTH), lambda bi, i: (bi, 0, i, 0))
    return pl.pallas_call(
        _out_cross_kernel,
        grid=(b, s // tm),
        in_specs=[tok(D_MODEL)] + [res(t) for t in flat] + [tok(B_WIDTH), tok(C_WIDTH),
                  _const_spec((D_MODEL, D_MODEL)), _const_spec((1, D_MODEL)), _const_spec((D_MODEL, X_WIDTH)),
                  per_b((X_WIDTH, N_MEM)), per_b((N_MEM, X_WIDTH)), _const_spec((X_WIDTH, D_MODEL))],
        out_specs=tok(D_MODEL),
        out_shape=jax.ShapeDtypeStruct((b, s, D_MODEL), F32),
        scratch_shapes=[pltpu.VMEM((4 * A_WIDTH // LANES, tm, LANES), F32)],
        compiler_params=_params(("arbitrary", "arbitrary")),
        name="out_cross",
    )(x, *flat, yb, yc, wo, gx, wq, kt, vm, wxo)


def _conv_ffn_kernel(x_ref, xp_ref, xn_ref, g_ref, wg_ref, wv_ref, cw_ref, cb_ref, wd_ref, fg_ref, out_ref, act_ref,
                     *, final_norm):
    i = pl.program_id(1)
    last = pl.num_programs(1) - 1
    g = g_ref[...]
    x = x_ref[0]
    tm = x.shape[0]
    hn = _rms(x, g).astype(BF16)
    hp = (_rms(xp_ref[0], g) * (i > 0).astype(F32)).astype(BF16)
    hx = (_rms(xn_ref[0], g) * (i < last).astype(F32)).astype(BF16)
    h_ext = jnp.concatenate([hp, hn, hx], axis=0)
    for c in range(D_FF_PAD // FF_CHUNK):
        cols = slice(c * FF_CHUNK, (c + 1) * FF_CHUNK)
        zg = _dot(h_ext, wg_ref[:, cols])
        ext = tm + 2 * SUBLANES
        zp = pltpu.roll(zg, 1, axis=0)[SUBLANES:SUBLANES + tm]
        zn = pltpu.roll(zg, ext - 1, axis=0)[SUBLANES:SUBLANES + tm]
        conv = (zp * cw_ref[0:1, cols] + zg[SUBLANES:SUBLANES + tm] * cw_ref[1:2, cols]
                + zn * cw_ref[2:3, cols] + cb_ref[:, cols])
        act_ref[:, cols] = (conv * jax.nn.sigmoid(conv) * _dot(hn, wv_ref[:, cols])).astype(BF16)
    y = x + _dot(act_ref[...], wd_ref[...])
    if final_norm:
        y = _rms(y, fg_ref[...])
    out_ref[0] = y


def _conv_ffn(x, g, wg, wv, cw, cb, wd, fg, tm, final_norm):
    b, s, _ = x.shape
    hb = tm // SUBLANES
    tok = pl.BlockSpec((1, tm, D_MODEL), lambda bi, i: (bi, i, 0))
    once = lambda shape: pl.BlockSpec(shape, lambda *_: (0,) * len(shape), pipeline_mode=pl.Buffered(1))
    return pl.pallas_call(
        functools.partial(_conv_ffn_kernel, final_norm=final_norm),
        grid=(b, s // tm),
        in_specs=[tok,
                  pl.BlockSpec((1, SUBLANES, D_MODEL), lambda bi, i: (bi, jnp.maximum(i * hb - 1, 0), 0)),
                  pl.BlockSpec((1, SUBLANES, D_MODEL), lambda bi, i: (bi, jnp.minimum((i + 1) * hb, s // SUBLANES - 1), 0)),
                  _const_spec((1, D_MODEL)), once((D_MODEL, D_FF_PAD)), once((D_MODEL, D_FF_PAD)),
                  _const_spec((3, D_FF_PAD)), _const_spec((1, D_FF_PAD)), once((D_FF_PAD, D_MODEL)),
                  _const_spec((1, D_MODEL))],
        out_specs=tok,
        out_shape=jax.ShapeDtypeStruct((b, s, D_MODEL), F32),
        scratch_shapes=[pltpu.VMEM((tm, D_FF_PAD), BF16)],
        compiler_params=_params(("arbitrary", "arbitrary")),
        name="conv_ffn",
    )(x, x, x, g, wg, wv, cw, cb, wd, fg)


def _rope_tables(positions, dim, theta):
    half = dim // 2
    s = positions[0].shape[0]
    inv = theta ** (-jnp.arange(0, dim, 2, dtype=F32) / dim)
    zeros = jnp.zeros((s, half), F32)
    c, sm, sp = [], [], []
    for pos in positions:
        ang = pos.astype(F32)[:, None] * inv[None, :]
        cos, sin = jnp.cos(ang), jnp.sin(ang)
        c += [cos, cos]
        sm += [-sin, zeros]
        sp += [zeros, sin]
    rest = HEAD_DIM - dim * len(positions)
    if rest:
        c.append(jnp.ones((s, rest), F32))
        sm.append(jnp.zeros((s, rest), F32))
        sp.append(jnp.zeros((s, rest), F32))
    two = lambda parts: jnp.concatenate(parts + parts, axis=1)
    return jnp.stack([two(c), two(sm), two(sp)], axis=0)


def _block_diag_mean(width):
    blk = np.arange(width) // HEAD_DIM
    return jnp.asarray((blk[:, None] == blk[None, :]).astype(np.float32) / HEAD_DIM, BF16)


def _prep_layer(p, l):
    splits = np.cumsum((A_WIDTH, A_WIDTH, A_WIDTH, 2 * B_WIDTH, B_WIDTH, B_WIDTH, 4 * B_HEADS,
                        C_WIDTH, C_KV_WIDTH, C_KV_WIDTH))[:-1].tolist()
    a_q, a_k, a_v, b_qk, b_v, b_o, b_g, c_q, c_k, c_v = jnp.split(p["w_in"][l], splits, axis=1)
    g4 = b_g.reshape(D_MODEL, 2, 2, B_HEADS)
    gates = jnp.concatenate([g4[:, :, 0].reshape(D_MODEL, N_GATE), g4[:, :, 1].reshape(D_MODEL, N_GATE),
                             jnp.zeros((D_MODEL, LANES - 2 * N_GATE), F32)], axis=1)
    w_in = jnp.concatenate([a_q * (Q_SCALE * LOG2_E), a_k, a_v, b_qk, b_v, b_o, c_q, c_k, c_v, gates], axis=1).astype(BF16)
    gbias = jnp.concatenate([p["mlstm_igate_b"][l].reshape(N_GATE), p["mlstm_fgate_b"][l].reshape(N_GATE),
                             jnp.zeros((LANES - 2 * N_GATE,), F32)]).reshape(1, LANES)
    kscale = jnp.concatenate([jnp.ones((B_WIDTH,), F32), jnp.full((B_WIDTH,), Q_SCALE, F32)]).reshape(1, 2 * B_WIDTH)
    pad_c = lambda w: jnp.pad(w, ((0, 0), (0, D_FF_PAD - D_FF)))
    w_up = p["w_ffn_up"][l]
    return dict(
        g_mix=p["norm_mix_g"][l].reshape(1, D_MODEL), w_in=w_in,
        cw=p["mlstm_conv_w"][l], cb=p["mlstm_conv_b"][l].reshape(1, 2 * B_WIDTH), kscale=kscale, gbias=gbias,
        ng=p["mlstm_norm_g"][l].reshape(1, B_WIDTH),
        qg=jnp.tile(p["qk_norm_g"][l, 0] * (Q_SCALE * LOG2_E), C_HEADS).reshape(1, C_WIDTH),
        kg=jnp.tile(p["qk_norm_g"][l, 1], C_KV_HEADS).reshape(1, C_KV_WIDTH),
        w_out=p["w_out"][l].astype(BF16),
        g_x=p["norm_x_g"][l].reshape(1, D_MODEL), g_mem=p["norm_mem_g"][l].reshape(1, D_MODEL),
        w_xq=(p["w_xq"][l] * (Q_SCALE * LOG2_E)).astype(BF16), w_xkv=p["w_xkv"][l].astype(BF16), w_xo=p["w_xo"][l].astype(BF16),
        g_ffn=p["norm_ffn_g"][l].reshape(1, D_MODEL),
        w_gate=pad_c(w_up[:, :D_FF]).astype(BF16), w_val=pad_c(w_up[:, D_FF:]).astype(BF16),
        fcw=pad_c(p["ffn_conv_w"][l]), fcb=pad_c(p["ffn_conv_b"][l].reshape(1, D_FF)),
        w_down=jnp.pad(p["w_ffn_down"][l], ((0, D_FF_PAD - D_FF), (0, 0))).astype(BF16),
    )


def _trunk(x, mem, layers, final_g, consts):
    b, s, _ = x.shape
    pos = jnp.arange(s)
    ta = _rope_tables([pos], ROPE_DIMS, ROPE_THETA)
    tc = _rope_tables([pos // GRID_W, pos % GRID_W], HEAD_DIM // 2, AXIAL_THETA)
    depth = len(layers)
    for l, w in enumerate(layers):
        kt, vm = _mem_kv(mem, w["g_mem"], w["w_xkv"])
        a1, a4, a16, bqk, bv, bo, bg, cq, ckv = _in_proj(x, w["g_mix"], w["w_in"], w["cw"], w["cb"], w["kscale"],
                                                         ta, tc, w["qg"], w["kg"], consts["bd"], tm=512)
        branches = [_dilated_branch(a) for a in (a1, a4, a16)]
        yb = _mlstm(bqk, bv, bo, bg, w["gbias"], w["ng"], consts["tl"], consts["tu"], consts["bd"])
        yc = _gqa(cq, ckv)
        x = _out_cross(x, branches, yb, yc, w["w_out"], w["g_x"], w["w_xq"], kt, vm, w["w_xo"], tm=1024)
        x = _conv_ffn(x, w["g_ffn"], w["w_gate"], w["w_val"], w["fcw"], w["fcb"], w["w_down"], final_g, tm=512,
                      final_norm=(l == depth - 1))
    return x


def kernel(x_prompt, x_sample, mem_prompt, mem_sample, norm_mix_g, w_in, mlstm_conv_w, mlstm_conv_b, mlstm_igate_b, mlstm_fgate_b, mlstm_norm_g, qk_norm_g, w_out, norm_x_g, norm_mem_g, w_xq, w_xkv, w_xo, norm_ffn_g, w_ffn_up, ffn_conv_w, ffn_conv_b, w_ffn_down, final_norm_g):
    p = dict(norm_mix_g=norm_mix_g, w_in=w_in, mlstm_conv_w=mlstm_conv_w, mlstm_conv_b=mlstm_conv_b,
             mlstm_igate_b=mlstm_igate_b, mlstm_fgate_b=mlstm_fgate_b, mlstm_norm_g=mlstm_norm_g,
             qk_norm_g=qk_norm_g, w_out=w_out, norm_x_g=norm_x_g, norm_mem_g=norm_mem_g, w_xq=w_xq, w_xkv=w_xkv,
             w_xo=w_xo, norm_ffn_g=norm_ffn_g, w_ffn_up=w_ffn_up, ffn_conv_w=ffn_conv_w, ffn_conv_b=ffn_conv_b,
             w_ffn_down=w_ffn_down)
    layers = [_prep_layer(p, l) for l in range(w_in.shape[0])]
    tri = np.tril(np.ones((MLSTM_CHUNK, MLSTM_CHUNK), np.float32))
    consts = dict(bd=_block_diag_mean(MXU_TILE),
                  tl=jnp.asarray(tri, BF16), tu=jnp.asarray(tri.T, BF16))
    fg = final_norm_g.reshape(1, D_MODEL)
    return (_trunk(x_prompt, mem_prompt, layers, fg, consts), _trunk(x_sample, mem_sample, layers, fg, consts))
```

```python
import functools

import numpy as np
import jax
import jax.numpy as jnp
from jax import lax
from jax.experimental import pallas as pl
from jax.experimental.pallas import tpu as pltpu

F32 = jnp.float32
BF16 = jnp.bfloat16

LANES = 128
SUBLANES = 8
MXU_TILE = 256
VMEM_LIMIT_BYTES = 56 * 1024 * 1024

D_MODEL = 1024
HEAD_DIM = 64
A_HEADS = 4
B_HEADS = 4
C_HEADS = 8
C_KV_HEADS = 2
A_WIDTH = A_HEADS * HEAD_DIM
B_WIDTH = B_HEADS * HEAD_DIM
C_WIDTH = C_HEADS * HEAD_DIM
C_KV_WIDTH = C_KV_HEADS * HEAD_DIM
DILATED_PAIRS = ((128, 1), (512, 4), (2048, 16))
BAND = 64
ROPE_THETA = 500000.0
ROPE_DIMS = HEAD_DIM // 4
AXIAL_THETA = 10000.0
GRID_W = 64
N_MEM = 256
X_HEADS = 4
X_WIDTH = 256
D_FF = 2752
D_FF_PAD = 2816
RMS_EPS = 1e-6
NEG_INF = -1e30
Q_SCALE = HEAD_DIM ** -0.5
LOG2_E = 1.4426950408889634

COL_A = 0
COL_BQK = COL_A + 3 * A_WIDTH
COL_BV = COL_BQK + 2 * B_WIDTH
COL_BO = COL_BV + B_WIDTH
COL_CQ = COL_BO + B_WIDTH
COL_CK = COL_CQ + C_WIDTH
COL_CV = COL_CK + C_KV_WIDTH
COL_G = COL_CV + C_KV_WIDTH
IN_COLS = COL_G + LANES
N_GATE = 2 * B_HEADS

TQ_DIL = 128
DIL_UNROLL = 8
MLSTM_CHUNK = 256
TQ_GQA = 256
KC_GQA = 1024
FF_CHUNK = 256
OUT_CROSS_BLOCKS = 4


def _dot(a, b):
    return jnp.dot(a, b, preferred_element_type=F32)


def _dot_nt(a, b):
    return lax.dot_general(a, b, (((1,), (1,)), ((), ())), preferred_element_type=F32)


def _split2(x):
    hi = x.astype(BF16)
    return hi, (x - hi.astype(F32)).astype(BF16)


def _split3(x):
    hi = x.astype(BF16)
    r = x - hi.astype(F32)
    mid = r.astype(BF16)
    return hi, mid, (r - mid.astype(F32)).astype(BF16)


def _head_mean(xx, bd):
    w = xx.shape[1]
    tile = bd.shape[0]
    xb = xx.astype(BF16)
    if w <= tile:
        return _dot(xb, bd[:w, :w])
    return jnp.concatenate([_dot(xb[:, t:t + tile], bd) for t in range(0, w, tile)], axis=1)


def _rms(x, g):
    return x * lax.rsqrt(jnp.mean(x * x, axis=-1, keepdims=True) + RMS_EPS) * g


def _lane_iota(shape):
    return lax.broadcasted_iota(jnp.int32, shape, len(shape) - 1)


def _head_mask(width, h, dtype):
    lane = _lane_iota((1, width))
    return ((lane >= h * HEAD_DIM) & (lane < (h + 1) * HEAD_DIM)).astype(dtype)


def _rope(x, tab_ref, shift):
    w = x.shape[1]
    rep = w // LANES
    tile = lambda t: jnp.concatenate([t] * rep, axis=1) if rep > 1 else t
    c, sm, sp = tile(tab_ref[0]), tile(tab_ref[1]), tile(tab_ref[2])
    return x * c + pltpu.roll(x, w - shift, axis=1) * sm + pltpu.roll(x, shift, axis=1) * sp


def _shift_rows(z, prev_row, next_row):
    n = z.shape[0]
    row = lax.broadcasted_iota(jnp.int32, z.shape, 0)
    zp = jnp.where(row == 0, prev_row, pltpu.roll(z, 1, axis=0))
    zn = jnp.where(row == n - 1, next_row, pltpu.roll(z, n - 1, axis=0))
    return zp, zn


def _params(sem):
    return pltpu.CompilerParams(dimension_semantics=sem, vmem_limit_bytes=VMEM_LIMIT_BYTES)


def _lockstep(gens):
    gens = list(gens)
    results = [None] * len(gens)
    live = list(range(len(gens)))
    while live:
        still = []
        for i in live:
            try:
                next(gens[i])
                still.append(i)
            except StopIteration as done:
                results[i] = done.value
        live = still
    return results


def _const_spec(shape):
    n = len(shape)
    return pl.BlockSpec(shape, lambda *_: (0,) * n)


def _mem_kv_kernel(mem_ref, g_ref, w_ref, kt_ref, v_ref):
    hn = _rms(mem_ref[0], g_ref[...]).astype(BF16)
    kv = _dot(hn, w_ref[...])
    kt_ref[0] = kv[:, :X_WIDTH].T.astype(BF16)
    v_ref[0] = kv[:, X_WIDTH:].astype(BF16)


def _mem_kv(mem, g, w):
    b = mem.shape[0]
    return pl.pallas_call(
        _mem_kv_kernel,
        grid=(b,),
        in_specs=[pl.BlockSpec((1, N_MEM, D_MODEL), lambda i: (i, 0, 0)),
                  _const_spec((1, D_MODEL)), _const_spec((D_MODEL, 2 * X_WIDTH))],
        out_specs=[pl.BlockSpec((1, X_WIDTH, N_MEM), lambda i: (i, 0, 0)),
                   pl.BlockSpec((1, N_MEM, X_WIDTH), lambda i: (i, 0, 0))],
        out_shape=[jax.ShapeDtypeStruct((b, X_WIDTH, N_MEM), BF16),
                   jax.ShapeDtypeStruct((b, N_MEM, X_WIDTH), BF16)],
        compiler_params=_params(("arbitrary",)),
        name="mem_kv",
    )(mem, g, w)


def _in_proj_kernel(x_ref, xp_ref, xn_ref, g_ref, w_ref, cw_ref, cb_ref, ks_ref, ta_ref, tc_ref, qg_ref, kg_ref,
                    bd_ref, a1_ref, a4_ref, a16_ref, bqk_ref, bv_ref, bo_ref, bg_ref, cq_ref, ckv_ref, asc_ref):
    i = pl.program_id(1)
    last = pl.num_programs(1) - 1
    g = g_ref[...]
    hn = _rms(x_ref[0], g).astype(BF16)
    tm = hn.shape[0]

    zc = _dot(hn, w_ref[:, COL_CQ:COL_G])
    cq = zc[:, :C_WIDTH]
    cq = cq * lax.rsqrt(_head_mean(cq * cq, bd_ref[...]) + RMS_EPS) * qg_ref[...]
    cq_ref[0] = _rope(cq, tc_ref, HEAD_DIM // 4).astype(BF16)
    ck = zc[:, C_WIDTH:C_WIDTH + C_KV_WIDTH]
    ck = ck * lax.rsqrt(_head_mean(ck * ck, bd_ref[...]) + RMS_EPS) * kg_ref[...]
    ckv_ref[0, :, :C_KV_WIDTH] = _rope(ck, tc_ref, HEAD_DIM // 4).astype(BF16)
    ckv_ref[0, :, C_KV_WIDTH:] = zc[:, C_WIDTH + C_KV_WIDTH:].astype(BF16)

    za = _dot(hn, w_ref[:, COL_A:COL_BQK])
    av = jnp.concatenate([_rope(za[:, :2 * A_WIDTH], ta_ref, ROPE_DIMS // 2), za[:, 2 * A_WIDTH:]], axis=1)
    a1_ref[0, 0] = av.astype(BF16)
    for c in range(3 * A_WIDTH // LANES):
        asc_ref[c] = av[:, c * LANES:(c + 1) * LANES]
    for dil, ref in ((DILATED_PAIRS[1][1], a4_ref), (DILATED_PAIRS[2][1], a16_ref)):
        for r in range(dil):
            for c in range(3 * A_WIDTH // LANES):
                ref[0, r, :, c * LANES:(c + 1) * LANES] = asc_ref[c, pl.ds(r, tm // dil, stride=dil), :].astype(BF16)

    wb = w_ref[:, COL_BQK:COL_BV]
    zb = _dot(hn, wb)
    zprev = _dot(_rms(xp_ref[0], g).astype(BF16), wb)[SUBLANES - 1:SUBLANES] * (i > 0).astype(F32)
    znext = _dot(_rms(xn_ref[0], g).astype(BF16), wb)[0:1] * (i < last).astype(F32)
    zp, zn = _shift_rows(zb, zprev, znext)
    conv = zp * cw_ref[0:1] + zb * cw_ref[1:2] + zn * cw_ref[2:3] + cb_ref[...]
    bqk_ref[0] = (conv * jax.nn.sigmoid(conv) * ks_ref[...]).astype(BF16)
    zrest = _dot(hn, w_ref[:, COL_BV:COL_CQ])
    bv_ref[0] = zrest[:, :B_WIDTH].astype(BF16)
    bo_ref[0] = zrest[:, B_WIDTH:]
    bg_ref[0] = _dot(hn, w_ref[:, COL_G:IN_COLS])


def _in_proj(x, g, w, cw, cb, kscale, ta, tc, qg, kg, bd, tm):
    b, s, _ = x.shape
    nt = s // tm
    hb = tm // SUBLANES
    tok = lambda width: pl.BlockSpec((1, tm, width), lambda bi, i: (bi, i, 0))
    out_widths = (2 * B_WIDTH, B_WIDTH, B_WIDTH, LANES, C_WIDTH, 2 * C_KV_WIDTH)
    out_dtypes = (BF16, BF16, F32, F32, BF16, BF16)
    dils = [d for _, d in DILATED_PAIRS]
    a_specs = [pl.BlockSpec((1, d, tm // d, 3 * A_WIDTH), lambda bi, i: (bi, 0, i, 0)) for d in dils]
    a_shapes = [jax.ShapeDtypeStruct((b, d, s // d, 3 * A_WIDTH), BF16) for d in dils]
    return pl.pallas_call(
        _in_proj_kernel,
        grid=(b, nt),
        in_specs=[tok(D_MODEL),
                  pl.BlockSpec((1, SUBLANES, D_MODEL), lambda bi, i: (bi, jnp.maximum(i * hb - 1, 0), 0)),
                  pl.BlockSpec((1, SUBLANES, D_MODEL), lambda bi, i: (bi, jnp.minimum((i + 1) * hb, s // SUBLANES - 1), 0)),
                  _const_spec((1, D_MODEL)), _const_spec((D_MODEL, IN_COLS)),
                  _const_spec((3, 2 * B_WIDTH)), _const_spec((1, 2 * B_WIDTH)), _const_spec((1, 2 * B_WIDTH)),
                  pl.BlockSpec((3, tm, LANES), lambda bi, i: (0, i, 0)),
                  pl.BlockSpec((3, tm, LANES), lambda bi, i: (0, i, 0)),
                  _const_spec((1, C_WIDTH)), _const_spec((1, C_KV_WIDTH)), _const_spec((MXU_TILE, MXU_TILE))],
        out_specs=a_specs + [tok(wd) for wd in out_widths],
        out_shape=a_shapes + [jax.ShapeDtypeStruct((b, s, wd), dt) for wd, dt in zip(out_widths, out_dtypes)],
        scratch_shapes=[pltpu.VMEM((3 * A_WIDTH // LANES, tm, LANES), F32)],
        compiler_params=_params(("arbitrary", "arbitrary")),
        name="in_proj",
    )(x, x, x, g, w, cw, cb, kscale, ta, tc, qg, kg, bd)


def _dilated_kernel(a_ref, o_ref, l_ref, *, length, n_res, win):
    n_blk = length // TQ_DIL
    masks = [_head_mask(A_WIDTH, h, BF16) for h in range(A_HEADS)]
    lane = _lane_iota((1, A_WIDTH))
    rel = (lax.broadcasted_iota(jnp.int32, (1, TQ_DIL, win), 2)
           - lax.broadcasted_iota(jnp.int32, (1, TQ_DIL, win), 1))

    def pick(parts):
        out = parts[A_HEADS - 1]
        for h in range(A_HEADS - 2, -1, -1):
            out = jnp.where(lane < (h + 1) * HEAD_DIM, parts[h], out)
        return out

    for r in range(n_res):

        def block(jb, carry, r=r):
            if isinstance(jb, int):
                j0 = jb * TQ_DIL
                ws = min(max(j0 - BAND, 0), length - win)
            else:
                j0 = pl.multiple_of(jb * TQ_DIL, TQ_DIL)
                ws = pl.multiple_of(jnp.clip(j0 - BAND, 0, length - win), BAND)
            q = a_ref[0, r, pl.ds(j0, TQ_DIL), :A_WIDTH]
            kw = a_ref[0, r, pl.ds(ws, win), A_WIDTH:2 * A_WIDTH]
            vw = a_ref[0, r, pl.ds(ws, win), 2 * A_WIDTH:]
            q4 = jnp.concatenate([q * m for m in masks], axis=0)
            s = _dot_nt(q4, kw).reshape(A_HEADS, TQ_DIL, win)
            s = jnp.where(jnp.abs(rel + (ws - j0)) <= BAND, s, NEG_INF)
            m = jnp.max(s, axis=-1, keepdims=True)
            p = jnp.exp2(s - m)
            l = jnp.sum(p, axis=-1, keepdims=True)
            pn = (p * (1.0 / l)).astype(BF16).reshape(A_HEADS * TQ_DIL, win)
            o4 = _dot(pn, vw).reshape(A_HEADS, TQ_DIL, A_WIDTH)
            lse = m + jnp.log2(l)
            o_ref[0, r, pl.ds(j0, TQ_DIL), :] = pick([o4[h] for h in range(A_HEADS)])
            l_ref[0, r, pl.ds(j0, TQ_DIL), :] = pick(
                [jnp.broadcast_to(lse[h], (TQ_DIL, A_WIDTH)) for h in range(A_HEADS)])
            return carry

        if n_blk == 1:
            block(0, 0)
        else:
            lax.fori_loop(0, n_blk, block, 0, unroll=min(n_blk, DIL_UNROLL))


def _dilated_branch(a):
    b, dilation, length, _ = a.shape
    n_res = max(1, min(dilation, 2048 // length))
    win = min(length, TQ_DIL + 2 * BAND)
    out_spec = pl.BlockSpec((1, n_res, length, A_WIDTH), lambda bi, ri: (bi, ri, 0, 0))
    return pl.pallas_call(
        functools.partial(_dilated_kernel, length=length, n_res=n_res, win=win),
        grid=(b, dilation // n_res),
        in_specs=[pl.BlockSpec((1, n_res, length, 3 * A_WIDTH), lambda bi, ri: (bi, ri, 0, 0))],
        out_specs=[out_spec, out_spec],
        out_shape=[jax.ShapeDtypeStruct((b, dilation, length, A_WIDTH), F32)] * 2,
        compiler_params=_params(("arbitrary", "arbitrary")),
        name=f"dilated_d{dilation}",
    )(a)


def _cummax_rows(x, reverse):
    n = x.shape[0]
    row = lax.broadcasted_iota(jnp.int32, x.shape, 0)
    step = 1
    while step < n:
        if reverse:
            x = jnp.maximum(x, jnp.where(row < n - step, pltpu.roll(x, n - step, axis=0), NEG_INF))
        else:
            x = jnp.maximum(x, jnp.where(row >= step, pltpu.roll(x, step, axis=0), NEG_INF))
        step *= 2
    return x


def _mlstm_kernel(qk_ref, v_ref, o_ref, g_ref, gb_ref, ng_ref, tl_ref, tu_ref, bd_ref, out_ref,
                  hbuf_ref, c_ref, n_ref, m_ref):
    lc = MLSTM_CHUNK
    n_chunks = qk_ref.shape[1] // lc
    lane = _lane_iota((1, B_WIDTH))
    glane = _lane_iota((1, LANES))
    head_masks = [_head_mask(B_WIDTH, h, BF16) for h in range(B_HEADS)]
    row_i = lax.broadcasted_iota(jnp.int32, (lc, lc), 0)
    col_j = lax.broadcasted_iota(jnp.int32, (lc, lc), 1)
    feat = lax.broadcasted_iota(jnp.int32, (B_WIDTH, B_WIDTH), 0) // HEAD_DIM
    bd_mask = feat == lax.broadcasted_iota(jnp.int32, (B_WIDTH, B_WIDTH), 1) // HEAD_DIM
    nfeat = lax.broadcasted_iota(jnp.int32, (B_WIDTH, LANES), 0) // HEAD_DIM
    nlane = lax.broadcasted_iota(jnp.int32, (B_WIDTH, LANES), 1)

    head_of_lane = lax.broadcasted_iota(jnp.int32, (lc, LANES), 1) // HEAD_DIM

    def spread(cols):
        out = cols[B_HEADS - 1]
        for h in range(B_HEADS - 2, -1, -1):
            out = jnp.where(lane < (h + 1) * HEAD_DIM, cols[h], out)
        return out

    def chunk(c, reverse):
        base = B_HEADS if reverse else 0
        r0 = pl.multiple_of(c * lc, lc)
        q = qk_ref[0, pl.ds(r0, lc), :B_WIDTH]
        k = qk_ref[0, pl.ds(r0, lc), B_WIDTH:]
        v = v_ref[0, pl.ds(r0, lc), :]
        gates = g_ref[0, pl.ds(r0, lc), :] + gb_ref[...]
        ig = gates
        lsig = jnp.minimum(gates, 0.0) - jnp.log1p(jnp.exp(-jnp.abs(gates)))
        lf = pltpu.roll(lsig, LANES - N_GATE, axis=1)
        tri = (tu_ref if reverse else tl_ref)[...]
        hi, mid, lo = _split3(lf)
        bcol = _dot(tri, hi) + _dot(tri, mid) + _dot(tri, lo)
        k_t = k.astype(F32).T.astype(BF16)
        yield
        tot = jnp.sum(lf, axis=0, keepdims=True)
        rcol = ig - bcol
        gend = tot + rcol
        d = int(reverse)
        m_prev = m_ref[d]
        m_new = jnp.maximum(tot + m_prev, jnp.max(gend, axis=0, keepdims=True))
        wend = jnp.exp(gend - m_new)
        decay = jnp.exp(tot + m_prev - m_new)
        r2_t = (rcol * LOG2_E).T
        q_c = _dot(q, c_ref[d].astype(BF16))
        q_n = _dot(q, n_ref[d].astype(BF16))
        yield
        acol = -jnp.maximum(m_prev, _cummax_rows(rcol, reverse))
        wstate = jnp.exp(m_prev + acol)
        floor_ = jnp.exp(acol - bcol)
        a2 = acol * LOG2_E
        mask = (col_j >= row_i) if reverse else (col_j <= row_i)
        ws, sum_w = [], jnp.zeros((lc, LANES), F32)
        for h in range(B_HEADS):
            yield
            dh = base + h
            sc = _dot(q * head_masks[h], k_t)
            e = jnp.exp2(jnp.where(mask, a2[:, dh:dh + 1] + r2_t[dh:dh + 1, :], NEG_INF))
            yield
            w = e * sc
            sum_w = jnp.where(glane == dh, jnp.sum(w, axis=1, keepdims=True), sum_w)
            ws.append(w.astype(BF16))
        yield
        den = wstate * q_n + sum_w
        rinv = 1.0 / jnp.maximum(jnp.abs(den), floor_)
        def heads(t):
            if t.shape[0] != lc:
                return spread([t[:, base + h:base + h + 1] for h in range(B_HEADS)])
            return jnp.concatenate(
                [jnp.take_along_axis(t, base + tile * (LANES // HEAD_DIM) + head_of_lane, axis=1)
                 for tile in range(B_WIDTH // LANES)], axis=1)
        pv = _dot(jnp.concatenate(ws, axis=1), jnp.concatenate([v * hm for hm in head_masks], axis=0))
        upd = _dot(k_t, (v.astype(F32) * heads(wend)).astype(BF16))
        nupd = _dot(k_t, wend.astype(BF16))
        yield
        h_out = (heads(wstate) * q_c + pv) * heads(rinv)
        c_ref[d] = heads(decay) * c_ref[d] + jnp.where(bd_mask, upd, 0.0)
        n_ref[d] = decay * n_ref[d] + jnp.where(nfeat == nlane - base, nupd, 0.0)
        m_ref[d] = m_new
        return r0, h_out

    def both(t):
        return _lockstep([chunk(n_chunks - 1 - t if reverse else t, reverse) for reverse in (False, True)])

    def finish(r0, hsum):
        hn = hsum * lax.rsqrt(_head_mean(hsum * hsum, bd_ref[...]) + RMS_EPS) * ng_ref[...]
        out_ref[0, pl.ds(r0, lc), :] = (jax.nn.sigmoid(o_ref[0, pl.ds(r0, lc), :]) * hn).astype(BF16)

    c_ref[...] = jnp.zeros_like(c_ref)
    n_ref[...] = jnp.zeros_like(n_ref)
    m_ref[...] = jnp.zeros_like(m_ref)

    def first_half(t, carry):
        for r0, h_out in both(t):
            hbuf_ref[pl.ds(r0, lc), :] = h_out
        return carry

    def second_half(t, carry):
        for r0, h_out in both(t):
            finish(r0, h_out + hbuf_ref[pl.ds(r0, lc), :])
        return carry

    lax.fori_loop(0, n_chunks // 2, first_half, 0)
    lax.fori_loop(n_chunks // 2, n_chunks, second_half, 0)


def _mlstm(bqk, bv, bo, bg, gbias, ng, tl, tu, bd):
    b, s, _ = bqk.shape
    assert s % (2 * MLSTM_CHUNK) == 0
    row = lambda width: pl.BlockSpec((1, s, width), lambda i: (i, 0, 0))
    return pl.pallas_call(
        _mlstm_kernel,
        grid=(b,),
        in_specs=[row(2 * B_WIDTH), row(B_WIDTH), row(B_WIDTH), row(LANES),
                  _const_spec((1, LANES)), _const_spec((1, B_WIDTH)),
                  _const_spec((MLSTM_CHUNK, MLSTM_CHUNK)), _const_spec((MLSTM_CHUNK, MLSTM_CHUNK)),
                  _const_spec((MXU_TILE, MXU_TILE))],
        out_specs=row(B_WIDTH),
        out_shape=jax.ShapeDtypeStruct((b, s, B_WIDTH), BF16),
        scratch_shapes=[pltpu.VMEM((s, B_WIDTH), F32), pltpu.VMEM((2, B_WIDTH, B_WIDTH), F32),
                        pltpu.VMEM((2, B_WIDTH, LANES), F32), pltpu.VMEM((2, 1, LANES), F32)],
        compiler_params=_params(("arbitrary",)),
        name="mlstm",
    )(bqk, bv, bo, bg, gbias, ng, tl, tu, bd)


def _gqa_kernel(q_ref, kv_ref, o_ref, kt_ref, vx_ref):
    s = kv_ref.shape[1]
    n_kc = s // KC_GQA
    lo = _lane_iota((1, LANES)) < HEAD_DIM
    lo_b = lo.astype(BF16)
    hi_b = 1.0 - lo_b

    @pl.when(pl.program_id(1) == 0)
    def _():
        def prep(c, carry):
            r0 = pl.multiple_of(c * KC_GQA, KC_GQA)
            kc = kv_ref[0, pl.ds(r0, KC_GQA), :C_KV_WIDTH].astype(F32)
            kr = pltpu.roll(kc, HEAD_DIM, axis=1)
            kt_ref[0, c] = jnp.where(lo, kc, kr).T.astype(BF16)
            kt_ref[1, c] = jnp.where(lo, kr, kc).T.astype(BF16)
            vc = kv_ref[0, pl.ds(r0, KC_GQA), C_KV_WIDTH:].astype(F32)
            vr = pltpu.roll(vc, HEAD_DIM, axis=1)
            vx_ref[0, pl.ds(r0, KC_GQA), :] = jnp.where(lo, vc, 0.0).astype(BF16)
            vx_ref[1, pl.ds(r0, KC_GQA), :] = jnp.where(lo, 0.0, vr).astype(BF16)
            vx_ref[2, pl.ds(r0, KC_GQA), :] = jnp.where(lo, vr, 0.0).astype(BF16)
            vx_ref[3, pl.ds(r0, KC_GQA), :] = jnp.where(lo, 0.0, vc).astype(BF16)
            return carry

        lax.fori_loop(0, n_kc, prep, 0)

    tq = q_ref.shape[1]
    per_grp = C_HEADS // C_KV_HEADS
    for grp in range(C_KV_HEADS):
        pairs = [grp * per_grp // 2 + j for j in range(per_grp // 2)]
        qs = []
        for pair in pairs:
            qp = q_ref[0, :, pair * LANES:(pair + 1) * LANES]
            qs += [qp * lo_b, qp * hi_b]

        def body(c, carry, qs=qs, grp=grp):
            ms, ls, accs = carry
            kt = kt_ref[grp, c]
            r0 = pl.multiple_of(c * KC_GQA, KC_GQA)
            new_m, new_l, alphas, ps, new_acc = [], [], [], [], []
            sc_next = _dot(qs[0], kt)
            for h in range(per_grp):
                sc = sc_next
                if h + 1 < per_grp:
                    sc_next = _dot(qs[h + 1], kt)
                m_new = jnp.maximum(ms[h], jnp.max(sc, axis=1, keepdims=True))
                alpha = jnp.exp2(ms[h] - m_new)
                p = jnp.exp2(sc - m_new)
                new_m.append(m_new)
                new_l.append(alpha * ls[h] + jnp.sum(p, axis=1, keepdims=True))
                alphas.append(alpha)
                ps.append(p.astype(BF16))
                if h % 2 == 1:
                    j = h // 2
                    scale = jnp.where(lo, alphas[2 * j], alphas[2 * j + 1])
                    new_acc.append(scale * accs[j]
                                   + _dot(ps[2 * j], vx_ref[2 * grp, pl.ds(r0, KC_GQA), :])
                                   + _dot(ps[2 * j + 1], vx_ref[2 * grp + 1, pl.ds(r0, KC_GQA), :]))
            return tuple(new_m), tuple(new_l), tuple(new_acc)

        init = (tuple(jnp.full((tq, 1), NEG_INF, F32) for _ in range(per_grp)),
                tuple(jnp.zeros((tq, 1), F32) for _ in range(per_grp)),
                tuple(jnp.zeros((tq, LANES), F32) for _ in range(per_grp // 2)))
        _, ls, accs = lax.fori_loop(0, n_kc, body, init, unroll=True)
        for j, pair in enumerate(pairs):
            o_ref[0, :, pair * LANES:(pair + 1) * LANES] = (
                accs[j] / jnp.where(lo, ls[2 * j], ls[2 * j + 1])).astype(BF16)


def _gqa(cq, ckv):
    b, s, _ = cq.shape
    return pl.pallas_call(
        _gqa_kernel,
        grid=(b, s // TQ_GQA),
        in_specs=[pl.BlockSpec((1, TQ_GQA, C_WIDTH), lambda bi, i: (bi, i, 0)),
                  pl.BlockSpec((1, s, 2 * C_KV_WIDTH), lambda bi, i: (bi, 0, 0))],
        out_specs=pl.BlockSpec((1, TQ_GQA, C_WIDTH), lambda bi, i: (bi, i, 0)),
        out_shape=jax.ShapeDtypeStruct((b, s, C_WIDTH), BF16),
        scratch_shapes=[pltpu.VMEM((C_KV_HEADS, s // KC_GQA, LANES, KC_GQA), BF16),
                        pltpu.VMEM((2 * C_KV_HEADS, s, LANES), BF16)],
        compiler_params=_params(("arbitrary", "arbitrary")),
        name="gqa",
    )(cq, ckv)


def _out_cross_kernel(x_ref, o1_ref, l1_ref, o2_ref, l2_ref, o3_ref, l3_ref, yb_ref, yc_ref, wo_ref,
                      gx_ref, wq_ref, mem_ref, gm_ref, wkv_ref, wxo_ref, out_ref, il_ref, kt_ref, vm_ref):
    tm = x_ref.shape[1]

    @pl.when(pl.program_id(1) == 0)
    def _():
        kv = _dot(_rms(mem_ref[0], gm_ref[...]).astype(BF16), wkv_ref[...])
        kt_ref[0] = kv[:, :X_WIDTH].T.astype(BF16)
        vm_ref[0] = kv[:, X_WIDTH:].astype(BF16)

    def tokens(ref, slot):
        d = ref.shape[1]
        if d == 1:
            return ref[0, 0]
        tiles = A_WIDTH // LANES
        for r in range(d):
            for c in range(tiles):
                il_ref[slot * tiles + c, pl.ds(r, tm // d, stride=d), :] = ref[0, r, :, c * LANES:(c + 1) * LANES]
        return jnp.concatenate([il_ref[slot * tiles + c] for c in range(tiles)], axis=1)

    l1, l2, l3 = tokens(l1_ref, 0), tokens(l2_ref, 0), tokens(l3_ref, 1)
    o1, o2, o3 = tokens(o1_ref, 0), tokens(o2_ref, 2), tokens(o3_ref, 3)
    lane = _lane_iota((1, X_WIDTH))
    masks = [_head_mask(X_WIDTH, h, BF16) for h in range(X_HEADS)]

    def row_block(rows):
        rb = rows.stop - rows.start
        mx = jnp.maximum(jnp.maximum(l1[rows], l2[rows]), l3[rows])
        e1, e2, e3 = jnp.exp2(l1[rows] - mx), jnp.exp2(l2[rows] - mx), jnp.exp2(l3[rows] - mx)
        ya = (e1 * o1[rows] + e2 * o2[rows] + e3 * o3[rows]) / (e1 + e2 + e3)
        cat = jnp.concatenate([ya.astype(BF16), yb_ref[0, rows, :], yc_ref[0, rows, :]], axis=1)
        x1 = x_ref[0, rows, :] + _dot(cat, wo_ref[...])
        yield
        q = _dot(_rms(x1, gx_ref[...]).astype(BF16), wq_ref[...]).astype(BF16)
        yield
        q4 = jnp.concatenate([q * m for m in masks], axis=0)
        s = _dot(q4, kt_ref[0])
        yield
        m = jnp.max(s, axis=1, keepdims=True)
        p = jnp.exp2(s - m)
        pn = (p * (1.0 / jnp.sum(p, axis=1, keepdims=True))).astype(BF16)
        o4 = _dot(pn, vm_ref[0])
        yield
        o = o4[(X_HEADS - 1) * rb:]
        for h in range(X_HEADS - 2, -1, -1):
            o = jnp.where(lane < (h + 1) * HEAD_DIM, o4[h * rb:(h + 1) * rb], o)
        out_ref[0, rows, :] = x1 + _dot(o.astype(BF16), wxo_ref[...])

    rb = tm // OUT_CROSS_BLOCKS
    _lockstep([row_block(slice(r0, r0 + rb)) for r0 in range(0, tm, rb)])


def _out_cross(x, branches, yb, yc, wo, gx, wq, mem, gm, wkv, wxo, tm):
    b, s, _ = x.shape
    tok = lambda width: pl.BlockSpec((1, tm, width), lambda bi, i: (bi, i, 0))
    per_b = lambda shape: pl.BlockSpec((1,) + shape, lambda bi, i: (bi, 0, 0))
    flat = [t for pair in branches for t in pair]
    res = lambda t: pl.BlockSpec((1, t.shape[1], tm // t.shape[1], A_WIDTH), lambda bi, i: (bi, 0, i, 0))
    return pl.pallas_call(
        _out_cross_kernel,
        grid=(b, s // tm),
        in_specs=[tok(D_MODEL)] + [res(t) for t in flat] + [tok(B_WIDTH), tok(C_WIDTH),
                  _const_spec((D_MODEL, D_MODEL)), _const_spec((1, D_MODEL)), _const_spec((D_MODEL, X_WIDTH)),
                  per_b((N_MEM, D_MODEL)), _const_spec((1, D_MODEL)), _const_spec((D_MODEL, 2 * X_WIDTH)),
                  _const_spec((X_WIDTH, D_MODEL))],
        out_specs=tok(D_MODEL),
        out_shape=jax.ShapeDtypeStruct((b, s, D_MODEL), F32),
        scratch_shapes=[pltpu.VMEM((4 * A_WIDTH // LANES, tm, LANES), F32),
                        pltpu.VMEM((1, X_WIDTH, N_MEM), BF16), pltpu.VMEM((1, N_MEM, X_WIDTH), BF16)],
        compiler_params=_params(("arbitrary", "arbitrary")),
        name="out_cross",
    )(x, *flat, yb, yc, wo, gx, wq, mem, gm, wkv, wxo)


def _conv_ffn_kernel(x_ref, xp_ref, xn_ref, g_ref, wg_ref, wv_ref, cw_ref, cb_ref, wd_ref, fg_ref, out_ref, act_ref,
                     *, final_norm):
    i = pl.program_id(1)
    last = pl.num_programs(1) - 1
    g = g_ref[...]
    x = x_ref[0]
    tm = x.shape[0]
    hn = _rms(x, g).astype(BF16)
    hp = (_rms(xp_ref[0], g) * (i > 0).astype(F32)).astype(BF16)
    hx = (_rms(xn_ref[0], g) * (i < last).astype(F32)).astype(BF16)
    h_ext = jnp.concatenate([hp, hn, hx], axis=0)
    for c in range(D_FF_PAD // FF_CHUNK):
        cols = slice(c * FF_CHUNK, (c + 1) * FF_CHUNK)
        zg = _dot(h_ext, wg_ref[:, cols])
        ext = tm + 2 * SUBLANES
        zp = pltpu.roll(zg, 1, axis=0)[SUBLANES:SUBLANES + tm]
        zn = pltpu.roll(zg, ext - 1, axis=0)[SUBLANES:SUBLANES + tm]
        conv = (zp * cw_ref[0:1, cols] + zg[SUBLANES:SUBLANES + tm] * cw_ref[1:2, cols]
                + zn * cw_ref[2:3, cols] + cb_ref[:, cols])
        act_ref[:, cols] = (conv * jax.nn.sigmoid(conv) * _dot(hn, wv_ref[:, cols])).astype(BF16)
    y = x + _dot(act_ref[...], wd_ref[...])
    if final_norm:
        y = _rms(y, fg_ref[...])
    out_ref[0] = y


def _conv_ffn(x, g, wg, wv, cw, cb, wd, fg, tm, final_norm):
    b, s, _ = x.shape
    hb = tm // SUBLANES
    tok = pl.BlockSpec((1, tm, D_MODEL), lambda bi, i: (bi, i, 0))
    once = lambda shape: pl.BlockSpec(shape, lambda *_: (0,) * len(shape), pipeline_mode=pl.Buffered(1))
    return pl.pallas_call(
        functools.partial(_conv_ffn_kernel, final_norm=final_norm),
        grid=(b, s // tm),
        in_specs=[tok,
                  pl.BlockSpec((1, SUBLANES, D_MODEL), lambda bi, i: (bi, jnp.maximum(i * hb - 1, 0), 0)),
                  pl.BlockSpec((1, SUBLANES, D_MODEL), lambda bi, i: (bi, jnp.minimum((i + 1) * hb, s // SUBLANES - 1), 0)),
                  _const_spec((1, D_MODEL)), once((D_MODEL, D_FF_PAD)), once((D_MODEL, D_FF_PAD)),
                  _const_spec((3, D_FF_PAD)), _const_spec((1, D_FF_PAD)), once((D_FF_PAD, D_MODEL)),
                  _const_spec((1, D_MODEL))],
        out_specs=tok,
        out_shape=jax.ShapeDtypeStruct((b, s, D_MODEL), F32),
        scratch_shapes=[pltpu.VMEM((tm, D_FF_PAD), BF16)],
        compiler_params=_params(("arbitrary", "arbitrary")),
        name="conv_ffn",
    )(x, x, x, g, wg, wv, cw, cb, wd, fg)


def _rope_tables(positions, dim, theta):
    half = dim // 2
    s = positions[0].shape[0]
    inv = theta ** (-jnp.arange(0, dim, 2, dtype=F32) / dim)
    zeros = jnp.zeros((s, half), F32)
    c, sm, sp = [], [], []
    for pos in positions:
        ang = pos.astype(F32)[:, None] * inv[None, :]
        cos, sin = jnp.cos(ang), jnp.sin(ang)
        c += [cos, cos]
        sm += [-sin, zeros]
        sp += [zeros, sin]
    rest = HEAD_DIM - dim * len(positions)
    if rest:
        c.append(jnp.ones((s, rest), F32))
        sm.append(jnp.zeros((s, rest), F32))
        sp.append(jnp.zeros((s, rest), F32))
    two = lambda parts: jnp.concatenate(parts + parts, axis=1)
    return jnp.stack([two(c), two(sm), two(sp)], axis=0)


def _block_diag_mean(width):
    blk = np.arange(width) // HEAD_DIM
    return jnp.asarray((blk[:, None] == blk[None, :]).astype(np.float32) / HEAD_DIM, BF16)


def _prep_layer(p, l):
    splits = np.cumsum((A_WIDTH, A_WIDTH, A_WIDTH, 2 * B_WIDTH, B_WIDTH, B_WIDTH, 4 * B_HEADS,
                        C_WIDTH, C_KV_WIDTH, C_KV_WIDTH))[:-1].tolist()
    a_q, a_k, a_v, b_qk, b_v, b_o, b_g, c_q, c_k, c_v = jnp.split(p["w_in"][l], splits, axis=1)
    g4 = b_g.reshape(D_MODEL, 2, 2, B_HEADS)
    gates = jnp.concatenate([g4[:, :, 0].reshape(D_MODEL, N_GATE), g4[:, :, 1].reshape(D_MODEL, N_GATE),
                             jnp.zeros((D_MODEL, LANES - 2 * N_GATE), F32)], axis=1)
    w_in = jnp.concatenate([a_q * (Q_SCALE * LOG2_E), a_k, a_v, b_qk, b_v, b_o, c_q, c_k, c_v, gates], axis=1).astype(BF16)
    gbias = jnp.concatenate([p["mlstm_igate_b"][l].reshape(N_GATE), p["mlstm_fgate_b"][l].reshape(N_GATE),
                             jnp.zeros((LANES - 2 * N_GATE,), F32)]).reshape(1, LANES)
    kscale = jnp.concatenate([jnp.ones((B_WIDTH,), F32), jnp.full((B_WIDTH,), Q_SCALE, F32)]).reshape(1, 2 * B_WIDTH)
    pad_c = lambda w: jnp.pad(w, ((0, 0), (0, D_FF_PAD - D_FF)))
    w_up = p["w_ffn_up"][l]
    return dict(
        g_mix=p["norm_mix_g"][l].reshape(1, D_MODEL), w_in=w_in,
        cw=p["mlstm_conv_w"][l], cb=p["mlstm_conv_b"][l].reshape(1, 2 * B_WIDTH), kscale=kscale, gbias=gbias,
        ng=p["mlstm_norm_g"][l].reshape(1, B_WIDTH),
        qg=jnp.tile(p["qk_norm_g"][l, 0] * (Q_SCALE * LOG2_E), C_HEADS).reshape(1, C_WIDTH),
        kg=jnp.tile(p["qk_norm_g"][l, 1], C_KV_HEADS).reshape(1, C_KV_WIDTH),
        w_out=p["w_out"][l].astype(BF16),
        g_x=p["norm_x_g"][l].reshape(1, D_MODEL), g_mem=p["norm_mem_g"][l].reshape(1, D_MODEL),
        w_xq=(p["w_xq"][l] * (Q_SCALE * LOG2_E)).astype(BF16), w_xkv=p["w_xkv"][l].astype(BF16), w_xo=p["w_xo"][l].astype(BF16),
        g_ffn=p["norm_ffn_g"][l].reshape(1, D_MODEL),
        w_gate=pad_c(w_up[:, :D_FF]).astype(BF16), w_val=pad_c(w_up[:, D_FF:]).astype(BF16),
        fcw=pad_c(p["ffn_conv_w"][l]), fcb=pad_c(p["ffn_conv_b"][l].reshape(1, D_FF)),
        w_down=jnp.pad(p["w_ffn_down"][l], ((0, D_FF_PAD - D_FF), (0, 0))).astype(BF16),
    )


def _trunk(x, mem, layers, final_g, consts):
    b, s, _ = x.shape
    pos = jnp.arange(s)
    ta = _rope_tables([pos], ROPE_DIMS, ROPE_THETA)
    tc = _rope_tables([pos // GRID_W, pos % GRID_W], HEAD_DIM // 2, AXIAL_THETA)
    depth = len(layers)
    for l, w in enumerate(layers):
        a1, a4, a16, bqk, bv, bo, bg, cq, ckv = _in_proj(x, w["g_mix"], w["w_in"], w["cw"], w["cb"], w["kscale"],
                                                         ta, tc, w["qg"], w["kg"], consts["bd"], tm=512)
        branches = [_dilated_branch(a) for a in (a1, a4, a16)]
        yb = _mlstm(bqk, bv, bo, bg, w["gbias"], w["ng"], consts["tl"], consts["tu"], consts["bd"])
        yc = _gqa(cq, ckv)
        x = _out_cross(x, branches, yb, yc, w["w_out"], w["g_x"], w["w_xq"], mem, w["g_mem"], w["w_xkv"], w["w_xo"], tm=1024)
        x = _conv_ffn(x, w["g_ffn"], w["w_gate"], w["w_val"], w["fcw"], w["fcb"], w["w_down"], final_g, tm=512,
                      final_norm=(l == depth - 1))
    return x


def kernel(x_prompt, x_sample, mem_prompt, mem_sample, norm_mix_g, w_in, mlstm_conv_w, mlstm_conv_b, mlstm_igate_b, mlstm_fgate_b, mlstm_norm_g, qk_norm_g, w_out, norm_x_g, norm_mem_g, w_xq, w_xkv, w_xo, norm_ffn_g, w_ffn_up, ffn_conv_w, ffn_conv_b, w_ffn_down, final_norm_g):
    p = dict(norm_mix_g=norm_mix_g, w_in=w_in, mlstm_conv_w=mlstm_conv_w, mlstm_conv_b=mlstm_conv_b,
             mlstm_igate_b=mlstm_igate_b, mlstm_fgate_b=mlstm_fgate_b, mlstm_norm_g=mlstm_norm_g,
             qk_norm_g=qk_norm_g, w_out=w_out, norm_x_g=norm_x_g, norm_mem_g=norm_mem_g, w_xq=w_xq, w_xkv=w_xkv,
             w_xo=w_xo, norm_ffn_g=norm_ffn_g, w_ffn_up=w_ffn_up, ffn_conv_w=ffn_conv_w, ffn_conv_b=ffn_conv_b,
             w_ffn_down=w_ffn_down)
    layers = [_prep_layer(p, l) for l in range(w_in.shape[0])]
    tri = np.tril(np.ones((MLSTM_CHUNK, MLSTM_CHUNK), np.float32))
    consts = dict(bd=_block_diag_mean(MXU_TILE),
                  tl=jnp.asarray(tri, BF16), tu=jnp.asarray(tri.T, BF16))
    fg = final_norm_g.reshape(1, D_MODEL)
    return (_trunk(x_prompt, mem_prompt, layers, fg, consts), _trunk(x_sample, mem_sample, layers, fg, consts))
```
